```python
import math
import jax, jax.numpy as jnp
from jax import lax
import numpy as np

D_MODEL = 2048
BATCH = 8
SEQ = 4096
DEPTH = 2

GDN_HEADS = 16
GDN_HEAD_DIM = 128
GDN_WIDTH = GDN_HEADS * GDN_HEAD_DIM
GDN_CHUNK = 64
CONV_K = 4
CONV_CH = 3 * GDN_WIDTH
SWA_Q_HEADS = 32
SWA_KV_HEADS = 4
SWA_HEAD_DIM = 64
SWA_GROUP = SWA_Q_HEADS // SWA_KV_HEADS
SWA_WIDTH = SWA_Q_HEADS * SWA_HEAD_DIM
SWA_KV_WIDTH = SWA_KV_HEADS * SWA_HEAD_DIM
WINDOW = 128
SWA_BLOCK = 128
ROPE_THETA = 10000.0
NORM_EPS = 1e-6
L2_EPS = 1e-6
SPLIT_SIZES = (CONV_CH, GDN_HEADS, GDN_HEADS, GDN_WIDTH,
               SWA_WIDTH, SWA_KV_WIDTH, SWA_KV_WIDTH, SWA_WIDTH,
               D_MODEL, D_MODEL)
IN_COLS = CONV_CH + 2 * GDN_HEADS + GDN_WIDTH + 2 * SWA_WIDTH + 2 * SWA_KV_WIDTH + 2 * D_MODEL

kernel_name = "hybrid_gdn_swa_sink_gated_merge_adaln"


def rmsnorm(x, w):
    xf = x.astype(jnp.float32)
    y = xf * lax.rsqrt(jnp.mean(xf * xf, axis=-1, keepdims=True) + NORM_EPS)
    return (y * w.astype(jnp.float32)).astype(x.dtype)


def l2norm(x):
    return x * lax.rsqrt(jnp.sum(x * x, axis=-1, keepdims=True) + L2_EPS)


def split_cols(u, sizes):
    outs, start = [], 0
    for s in sizes:
        outs.append(u[..., start:start + s])
        start += s
    return outs


def causal_conv(x, w):
    ch = x.shape[-1]
    return lax.conv_general_dilated(
        x, w.astype(x.dtype)[:, None, :], window_strides=(1,), padding=[(CONV_K - 1, 0)],
        dimension_numbers=("NWC", "WIO", "NWC"), feature_group_count=ch)


def rope(x, positions):
    half = x.shape[-1] // 2
    inv_freq = ROPE_THETA ** (-jnp.arange(half, dtype=jnp.float32) / half)
    ang = positions.astype(jnp.float32)[..., None] * inv_freq
    cos = jnp.cos(ang)[:, :, None, :]
    sin = jnp.sin(ang)[:, :, None, :]
    xf = x.astype(jnp.float32)
    x1, x2 = xf[..., :half], xf[..., half:]
    return jnp.concatenate([x1 * cos - x2 * sin, x2 * cos + x1 * sin], axis=-1).astype(x.dtype)


def gated_delta_rule(q, k, v, g, beta):
    b, s, h, dk = q.shape
    dv = v.shape[-1]
    n = s // GDN_CHUNK

    def to_chunks(t):
        t = t.reshape((b, n, GDN_CHUNK) + t.shape[2:])
        return jnp.moveaxis(t, 3, 1)

    q, k, v, beta = to_chunks(q), to_chunks(k), to_chunks(v), to_chunks(beta)
    g_cum = jnp.cumsum(to_chunks(g), axis=-1)
    idx = jnp.arange(GDN_CHUNK)
    causal = idx[:, None] >= idx[None, :]
    strict = idx[:, None] > idx[None, :]
    decay = jnp.exp(jnp.where(causal, g_cum[..., :, None] - g_cum[..., None, :], -jnp.inf))
    k_beta = k * beta[..., None]
    l_mat = jnp.where(strict, jnp.einsum("bhncd,bhnjd->bhncj", k_beta, k) * decay, 0.0)
    a_mat = l_mat + jnp.eye(GDN_CHUNK, dtype=l_mat.dtype)
    rhs = jnp.concatenate([v * beta[..., None], k_beta * jnp.exp(g_cum)[..., None]], axis=-1)
    sol = lax.linalg.triangular_solve(a_mat, rhs, left_side=True, lower=True, unit_diagonal=True)
    u, w = sol[..., :dv], sol[..., dv:]
    qk = jnp.einsum("bhncd,bhnjd->bhncj", q, k) * decay
    q_dec = q * jnp.exp(g_cum)[..., None]
    k_dec = k * jnp.exp(g_cum[..., -1:] - g_cum)[..., None]
    chunk_decay = jnp.exp(g_cum[..., -1])
    xs = tuple(jnp.moveaxis(t, 2, 0) for t in (q_dec, k_dec, u, w, qk, chunk_decay))

    def step(state, inp):
        q_c, k_c, u_c, w_c, qk_c, dec_c = inp
        v_new = u_c - jnp.einsum("bhcd,bhde->bhce", w_c, state)
        o_c = jnp.einsum("bhcd,bhde->bhce", q_c, state) + jnp.einsum("bhcj,bhje->bhce", qk_c, v_new)
        state = state * dec_c[..., None, None] + jnp.einsum("bhcd,bhce->bhde", k_c, v_new)
        return state, o_c

    state0 = jnp.zeros((b, h, dk, dv), jnp.float32)
    _, o = lax.scan(step, state0, xs)
    return jnp.transpose(o, (1, 0, 3, 2, 4)).reshape(b, s, h, dv)


def swa_with_sinks(q, k, v, sinks):
    b, s, hq, d = q.shape
    nb, t = s // SWA_BLOCK, SWA_BLOCK
    qb = q.reshape(b, nb, t, SWA_KV_HEADS, SWA_GROUP, d)
    kb = k.reshape(b, nb, t, SWA_KV_HEADS, d)
    vb = v.reshape(b, nb, t, SWA_KV_HEADS, d)
    pad = ((0, 0), (1, 0), (0, 0), (0, 0), (0, 0))
    keys = jnp.concatenate([jnp.pad(kb, pad)[:, :-1], kb], axis=2)
    vals = jnp.concatenate([jnp.pad(vb, pad)[:, :-1], vb], axis=2)
    scores = jnp.einsum("bnqhgd,bnkhd->bnhgqk", qb, keys,
                        preferred_element_type=jnp.float32) * (d ** -0.5)
    blk = jnp.arange(nb)[:, None]
    qpos = blk * t + jnp.arange(t)[None, :]
    kpos = blk * t - t + jnp.arange(2 * t)[None, :]
    diff = qpos[:, :, None] - kpos[:, None, :]
    valid = (diff >= 0) & (diff < WINDOW) & (kpos[:, None, :] >= 0)
    scores = jnp.where(valid[None, :, None, None], scores, -jnp.inf)
    sink = sinks.astype(jnp.float32).reshape(SWA_KV_HEADS, SWA_GROUP)[None, None, :, :, None, None]
    m = jnp.maximum(jnp.max(scores, axis=-1, keepdims=True), sink)
    p = jnp.exp(scores - m)
    probs = p / (jnp.sum(p, axis=-1, keepdims=True) + jnp.exp(sink - m))
    o = jnp.einsum("bnhgqk,bnkhd->bnqhgd", probs.astype(v.dtype), vals,
                   preferred_element_type=jnp.float32)
    return o.reshape(b, s, hq, d).astype(v.dtype)


def hybrid_layer(x, c_act, positions, ada_w, ada_b, norm_w, w_in, conv_w, a_log, dt_bias,
                 gdn_norm_w, sinks, proj_a, proj_b, w_out):
    b, s, _ = x.shape
    mod = c_act @ ada_w + ada_b
    shift, scale, gate = mod[:, :D_MODEL], mod[:, D_MODEL:2 * D_MODEL], mod[:, 2 * D_MODEL:]
    h = rmsnorm(x, norm_w) * (1.0 + scale[:, None, :]) + shift[:, None, :]
    u = h @ w_in
    qkv_a, beta_a, a_a, z_a, q_b, k_b, v_b, z_b, g_a, g_b = split_cols(u, SPLIT_SIZES)

    qkv_a = jax.nn.silu(causal_conv(qkv_a, conv_w)).astype(jnp.float32)
    q_a = qkv_a[..., :GDN_WIDTH].reshape(b, s, GDN_HEADS, GDN_HEAD_DIM)
    k_a = qkv_a[..., GDN_WIDTH:2 * GDN_WIDTH].reshape(b, s, GDN_HEADS, GDN_HEAD_DIM)
    v_a = qkv_a[..., 2 * GDN_WIDTH:].reshape(b, s, GDN_HEADS, GDN_HEAD_DIM)
    q_a = l2norm(q_a) * (GDN_HEAD_DIM ** -0.5)
    k_a = l2norm(k_a)
    beta = jax.nn.sigmoid(beta_a.astype(jnp.float32))
    g = -jnp.exp(a_log.astype(jnp.float32)) * jax.nn.softplus(
        a_a.astype(jnp.float32) + dt_bias.astype(jnp.float32))
    o_a = gated_delta_rule(q_a, k_a, v_a, g, beta)
    o_a = rmsnorm(o_a, gdn_norm_w).reshape(b, s, GDN_WIDTH).astype(x.dtype) * jax.nn.silu(z_a)

    q_b = rope(q_b.reshape(b, s, SWA_Q_HEADS, SWA_HEAD_DIM), positions)
    k_b = rope(k_b.reshape(b, s, SWA_KV_HEADS, SWA_HEAD_DIM), positions)
    v_b = v_b.reshape(b, s, SWA_KV_HEADS, SWA_HEAD_DIM)
    o_b = swa_with_sinks(q_b, k_b, v_b, sinks).reshape(b, s, SWA_WIDTH) * jax.nn.silu(z_b)

    y = jax.nn.sigmoid(g_a) * (o_a @ proj_a) + jax.nn.sigmoid(g_b) * (o_b @ proj_b)
    return x + gate[:, None, :] * (y @ w_out)


def setup_inputs(seed: int = 0) -> dict:
    key = jax.random.key(seed)
    ks = jax.random.split(key, 16)
    f32 = jnp.float32
    x = jax.random.normal(ks[0], (BATCH, SEQ, D_MODEL), f32)
    c = jax.random.normal(ks[1], (BATCH, D_MODEL), f32)
    offsets = jax.random.randint(ks[2], (BATCH, 1), 0, 1024, dtype=jnp.int32)
    positions = (offsets + jnp.arange(SEQ, dtype=jnp.int32)[None, :]).astype(jnp.int32)
    ada_w = jax.random.normal(ks[3], (DEPTH, D_MODEL, 3 * D_MODEL), f32) * D_MODEL ** -0.5
    ada_b = 0.02 * jax.random.normal(ks[4], (DEPTH, 3 * D_MODEL), f32)
    norm_w = 1.0 + 0.02 * jax.random.normal(ks[5], (DEPTH, D_MODEL), f32)
    w_in = jax.random.normal(ks[6], (DEPTH, D_MODEL, IN_COLS), f32) * D_MODEL ** -0.5
    conv_w = jax.random.normal(ks[7], (DEPTH, CONV_K, CONV_CH), f32) * CONV_K ** -0.5
    gdn_a_log = jnp.log(jax.random.uniform(ks[8], (DEPTH, GDN_HEADS), f32, 1.0, 16.0))
    dt = jnp.exp(jax.random.uniform(ks[9], (DEPTH, GDN_HEADS), f32, math.log(1e-3), math.log(1e-1)))
    gdn_dt_bias = dt + jnp.log(-jnp.expm1(-dt))
    gdn_norm_w = 1.0 + 0.02 * jax.random.normal(ks[10], (DEPTH, GDN_HEAD_DIM), f32)
    swa_sinks = 0.5 * jax.random.normal(ks[11], (DEPTH, SWA_Q_HEADS), f32)
    proj_a = jax.random.normal(ks[12], (DEPTH, GDN_WIDTH, D_MODEL), f32) * GDN_WIDTH ** -0.5
    proj_b = jax.random.normal(ks[13], (DEPTH, SWA_WIDTH, D_MODEL), f32) * SWA_WIDTH ** -0.5
    w_out = jax.random.normal(ks[14], (DEPTH, D_MODEL, D_MODEL), f32) * D_MODEL ** -0.5
    final_norm_w = 1.0 + 0.02 * jax.random.normal(ks[15], (D_MODEL,), f32)
    return {"x": x, "c": c, "positions": positions, "ada_w": ada_w, "ada_b": ada_b,
            "norm_w": norm_w, "w_in": w_in, "conv_w": conv_w, "gdn_a_log": gdn_a_log,
            "gdn_dt_bias": gdn_dt_bias, "gdn_norm_w": gdn_norm_w, "swa_sinks": swa_sinks,
            "proj_a": proj_a, "proj_b": proj_b, "w_out": w_out, "final_norm_w": final_norm_w}


def reference(x, c, positions, ada_w, ada_b, norm_w, w_in, conv_w, gdn_a_log, gdn_dt_bias,
              gdn_norm_w, swa_sinks, proj_a, proj_b, w_out, final_norm_w):
    c_act = jax.nn.silu(c)
    for l in range(DEPTH):
        x = hybrid_layer(x, c_act, positions, ada_w[l], ada_b[l], norm_w[l], w_in[l], conv_w[l],
                         gdn_a_log[l], gdn_dt_bias[l], gdn_norm_w[l], swa_sinks[l],
                         proj_a[l], proj_b[l], w_out[l])
    return rmsnorm(x, final_norm_w)
```

```python
import functools

import jax
import jax.numpy as jnp
from jax import lax
from jax.experimental import pallas as pl
from jax.experimental.pallas import tpu as pltpu

F32 = jnp.float32
BF16 = jnp.bfloat16

LANES = 128
GDN_HEAD_DIM = 128
GDN_CHUNK = 64
CONV_K = 4
CONV_HALO = 8
SWA_HEAD_DIM = 64
SWA_BLOCK = 128
ROPE_THETA = 10000.0
NORM_EPS = 1e-6
L2_EPS = 1e-6
MASK_VALUE = -1e30
VMEM_LIMIT = 56 * 1024 * 1024


def _params(semantics):
    return pltpu.CompilerParams(dimension_semantics=semantics, vmem_limit_bytes=VMEM_LIMIT)


def _mm(a, b):
    return jnp.dot(a.astype(BF16), b.astype(BF16), preferred_element_type=F32)


def _mm_nt(a, b):
    return lax.dot_general(a.astype(BF16), b.astype(BF16), (((1,), (1,)), ((), ())),
                           preferred_element_type=F32)


def _mm_tn(a, b):
    return lax.dot_general(a.astype(BF16), b.astype(BF16), (((0,), (0,)), ((), ())),
                           preferred_element_type=F32)


def _silu(x):
    return x * jax.nn.sigmoid(x)


def _softplus(x):
    return jnp.maximum(x, 0.0) + jnp.log1p(jnp.exp(-jnp.abs(x)))


def _mod_kernel(c_ref, w_ref, b_ref, o_ref):
    c = c_ref[...]
    act = _silu(c)
    hi = act.astype(BF16)
    lo = (act - hi.astype(F32)).astype(BF16)
    w = w_ref[0].astype(BF16)
    acc = jnp.dot(hi, w, preferred_element_type=F32) + jnp.dot(lo, w, preferred_element_type=F32)
    o_ref[0] = acc + b_ref[0]


def _modulation(c, ada_w, ada_b, tn):
    depth, d, n = ada_w.shape
    b = c.shape[0]
    return pl.pallas_call(
        _mod_kernel,
        grid=(depth, n // tn),
        in_specs=[
            pl.BlockSpec((b, d), lambda l, j: (0, 0)),
            pl.BlockSpec((1, d, tn), lambda l, j: (l, 0, j)),
            pl.BlockSpec((1, 1, tn), lambda l, j: (l, 0, j)),
        ],
        out_specs=pl.BlockSpec((1, b, tn), lambda l, j: (l, 0, j)),
        out_shape=jax.ShapeDtypeStruct((depth, b, n), F32),
        compiler_params=_params(("arbitrary", "arbitrary")),
        name="adaln_modulation",
    )(c, ada_w, ada_b.reshape(depth, 1, n))


def _modulated_norm(x, nw, mod):
    y = x * lax.rsqrt(jnp.mean(x * x, axis=-1, keepdims=True) + NORM_EPS) * nw
    return y * (1.0 + mod[1:2, :]) + mod[0:1, :]


def _prep_kernel(x_ref, mod_ref, nw_ref, h_ref):
    h_ref[...] = _modulated_norm(x_ref[...], nw_ref[...], mod_ref[0]).astype(h_ref.dtype)


def _prep(x2, mod, nw, batch, tm):
    t, d = x2.shape
    nt = t // batch // tm
    return pl.pallas_call(
        _prep_kernel,
        grid=(batch, nt),
        in_specs=[
            pl.BlockSpec((tm, d), lambda b, i: (b * nt + i, 0)),
            pl.BlockSpec((1, 3, d), lambda b, i: (b, 0, 0)),
            pl.BlockSpec((1, d), lambda b, i: (0, 0)),
        ],
        out_specs=pl.BlockSpec((tm, d), lambda b, i: (b * nt + i, 0)),
        out_shape=jax.ShapeDtypeStruct((t, d), BF16),
        compiler_params=_params(("arbitrary", "arbitrary")),
        name="modulated_norm",
    )(x2, mod, nw)


def _matmul_kernel(a_ref, w_ref, o_ref):
    o_ref[...] = jnp.dot(a_ref[...], w_ref[...], preferred_element_type=F32).astype(o_ref.dtype)


def _in_proj_main(h, w, tm, tn):
    t, d = h.shape
    n = w.shape[1]
    return pl.pallas_call(
        _matmul_kernel,
        grid=(n // tn, t // tm),
        in_specs=[
            pl.BlockSpec((tm, d), lambda j, i: (i, 0)),
            pl.BlockSpec((d, tn), lambda j, i: (0, j)),
        ],
        out_specs=pl.BlockSpec((tm, tn), lambda j, i: (i, j)),
        out_shape=jax.ShapeDtypeStruct((t, n), BF16),
        compiler_params=_params(("arbitrary", "arbitrary")),
        name="in_proj_main",
    )(h, w)


def _tail_kernel(a_ref, w_ref, kv_ref, gb_ref, *, kv_cols):
    r = jnp.dot(a_ref[...], w_ref[...], preferred_element_type=F32)
    kv_ref[...] = r[:, :kv_cols].astype(kv_ref.dtype)
    gb_ref[...] = r[:, kv_cols:]


def _in_proj_tail(h, w, kv_cols, tm):
    t, d = h.shape
    n = w.shape[1]
    return pl.pallas_call(
        functools.partial(_tail_kernel, kv_cols=kv_cols),
        grid=(t // tm,),
        in_specs=[
            pl.BlockSpec((tm, d), lambda i: (i, 0)),
            pl.BlockSpec((d, n), lambda i: (0, 0)),
        ],
        out_specs=[
            pl.BlockSpec((tm, kv_cols), lambda i: (i, 0)),
            pl.BlockSpec((tm, n - kv_cols), lambda i: (i, 0)),
        ],
        out_shape=[
            jax.ShapeDtypeStruct((t, kv_cols), BF16),
            jax.ShapeDtypeStruct((t, n - kv_cols), F32),
        ],
        compiler_params=_params(("arbitrary",)),
        name="in_proj_tail",
    )(h, w)


INV_BASE = 8


def _inverse_masks(n):
    ii = lax.broadcasted_iota(jnp.int32, (n, n), 0)
    jj = lax.broadcasted_iota(jnp.int32, (n, n), 1)
    shift = INV_BASE.bit_length() - 1
    base = (ii >> shift) == (jj >> shift)
    levels = []
    blk = INV_BASE
    while blk < n:
        same_parent = (ii >> (shift + 1)) == (jj >> (shift + 1))
        levels.append(jnp.logical_and(same_parent, (ii >> shift) != (jj >> shift)))
        shift += 1
        blk *= 2
    return base, levels


def _unit_lower_inverse(lm, eye, masks):
    base, levels = masks
    d = jnp.where(base, lm, 0.0)
    x = eye - d
    p = d
    step = 2
    while step < INV_BASE:
        p = _mm(p, p)
        x = x + _mm(x, p)
        step *= 2
    for off_mask in levels:
        x = x - _mm(_mm(x, jnp.where(off_mask, lm, 0.0)), x)
    return x


def _chunk_cumsum(g):
    rows = lax.broadcasted_iota(jnp.int32, g.shape, 0)
    s = 1
    while s < g.shape[0]:
        g = g + jnp.where(rows >= s, pltpu.roll(g, s, 0), 0.0)
        s *= 2
    return g


def _gdn_kernel(q_ref, k_ref, v_ref, z_ref, gb_ref, cw_ref, hp_ref, nw_ref, o_ref,
                xpad_ref, state_ref, *, ts, heads):
    c = GDN_CHUNK
    hd = GDN_HEAD_DIM
    w = heads * hd
    t = pl.program_id(1)

    @pl.when(t == 0)
    def _():
        xpad_ref[0:CONV_HALO, :] = jnp.zeros((CONV_HALO, 3 * w), F32)
        state_ref[...] = jnp.zeros(state_ref.shape, F32)

    @pl.when(t > 0)
    def _():
        xpad_ref[0:CONV_HALO, :] = xpad_ref[ts:ts + CONV_HALO, :]

    xpad_ref[CONV_HALO:CONV_HALO + ts, 0:w] = q_ref[...].astype(F32)
    xpad_ref[CONV_HALO:CONV_HALO + ts, w:2 * w] = k_ref[...].astype(F32)
    xpad_ref[CONV_HALO:CONV_HALO + ts, 2 * w:3 * w] = v_ref[...].astype(F32)

    ii = lax.broadcasted_iota(jnp.int32, (c, c), 0)
    jj = lax.broadcasted_iota(jnp.int32, (c, c), 1)
    causal = ii >= jj
    strict = ii > jj
    eye = jnp.where(ii == jj, 1.0, 0.0).astype(F32)
    inv_masks = _inverse_masks(c)
    norm_w = nw_ref[...]

    for ci in range(ts // c):
        r0 = ci * c
        gb = gb_ref[r0:r0 + c, :]
        beta_all = jax.nn.sigmoid(gb)
        g_all = -jnp.exp(hp_ref[0:1, :]) * _softplus(gb + hp_ref[1:2, :])
        gc_all = _chunk_cumsum(g_all)
        gc_t = jnp.concatenate([gc_all, jnp.zeros((LANES - c, LANES), F32)], axis=0).T
        glast = gc_all[c - 1:c, :]
        egc_all = jnp.exp(gc_all)
        ekd_all = jnp.exp(glast - gc_all)
        cdec_all = jnp.exp(glast)

        for h in range(heads):
            def conv_silu(part):
                col = part * w + h * hd
                acc = None
                for j in range(CONV_K):
                    start = r0 + CONV_HALO - (CONV_K - 1) + j
                    term = xpad_ref[start:start + c, col:col + hd] * cw_ref[j:j + 1, col:col + hd]
                    acc = term if acc is None else acc + term
                return _silu(acc)

            q = conv_silu(0)
            k = conv_silu(1)
            v = conv_silu(2)
            q = q * lax.rsqrt(jnp.sum(q * q, axis=-1, keepdims=True) + L2_EPS) * (hd ** -0.5)
            k = k * lax.rsqrt(jnp.sum(k * k, axis=-1, keepdims=True) + L2_EPS)

            la = heads + h
            bcol = beta_all[:, h:h + 1]
            gcol = gc_all[:, la:la + 1]
            egcol = egc_all[:, la:la + 1]
            ekdcol = ekd_all[:, la:la + 1]
            cdec = cdec_all[:, la:la + 1]
            grow = gc_t[la:la + 1, 0:c]

            decay = jnp.where(causal, jnp.exp(jnp.where(causal, gcol - grow, 0.0)), 0.0)
            kb = k * bcol
            both = _mm_nt(jnp.concatenate([kb, q], axis=0), k)
            lm = jnp.where(strict, both[0:c] * decay, 0.0)
            qkm = both[c:2 * c] * decay
            tinv = _unit_lower_inverse(lm, eye, inv_masks)
            sol = _mm(tinv, jnp.concatenate([v * bcol, kb * egcol], axis=1))
            u = sol[:, 0:hd]
            wd = sol[:, hd:2 * hd]

            s_old = state_ref[h]
            v_new = u - _mm(wd, s_old)
            o = _mm(q * egcol, s_old) + _mm(qkm, v_new)
            state_ref[h] = s_old * cdec + _mm_tn(k * ekdcol, v_new)

            on = o * lax.rsqrt(jnp.mean(o * o, axis=-1, keepdims=True) + NORM_EPS) * norm_w
            z = z_ref[r0:r0 + c, h * hd:(h + 1) * hd].astype(F32)
            o_ref[r0:r0 + c, h * hd:(h + 1) * hd] = (on * _silu(z)).astype(o_ref.dtype)


def _gdn(u_main, gb, conv_w, head_params, norm_w, batch, heads, ts):
    t = u_main.shape[0]
    w = heads * GDN_HEAD_DIM
    nt = t // batch // ts

    def col(blk):
        return pl.BlockSpec((ts, w), lambda b, i, blk=blk: (b * nt + i, blk))

    return pl.pallas_call(
        functools.partial(_gdn_kernel, ts=ts, heads=heads),
        grid=(batch, nt),
        in_specs=[
            col(0), col(1), col(2), col(3),
            pl.BlockSpec((ts, LANES), lambda b, i: (b * nt + i, 0)),
            pl.BlockSpec((CONV_K, 3 * w), lambda b, i: (0, 0)),
            pl.BlockSpec((8, LANES), lambda b, i: (0, 0)),
            pl.BlockSpec((1, GDN_HEAD_DIM), lambda b, i: (0, 0)),
        ],
        out_specs=pl.BlockSpec((ts, w), lambda b, i: (b * nt + i, 0)),
        out_shape=jax.ShapeDtypeStruct((t, w), BF16),
        scratch_shapes=[
            pltpu.VMEM((ts + CONV_HALO, 3 * w), F32),
            pltpu.VMEM((heads, GDN_HEAD_DIM, GDN_HEAD_DIM), F32),
        ],
        compiler_params=_params(("arbitrary", "arbitrary")),
        name="gated_deltanet",
    )(u_main, u_main, u_main, u_main, gb, conv_w, head_params, norm_w)


def _rope_table_kernel(pos_ref, invf_ref, sign_ref, cos_ref, sin_ref):
    ang = pos_ref[...].astype(F32) * invf_ref[...]
    cos_ref[...] = jnp.cos(ang)
    sin_ref[...] = jnp.sin(ang) * sign_ref[...]


def _rope_tables(positions, tr):
    t = positions.size
    half = SWA_HEAD_DIM // 2
    inv_freq = ROPE_THETA ** (-jnp.arange(half, dtype=F32) / half)
    invf = jnp.tile(inv_freq, LANES // half).reshape(1, LANES)
    sign = jnp.tile(jnp.concatenate([-jnp.ones((half,), F32), jnp.ones((half,), F32)]),
                    LANES // SWA_HEAD_DIM).reshape(1, LANES)
    return pl.pallas_call(
        _rope_table_kernel,
        grid=(t // tr,),
        in_specs=[
            pl.BlockSpec((tr, 1), lambda i: (i, 0)),
            pl.BlockSpec((1, LANES), lambda i: (0, 0)),
            pl.BlockSpec((1, LANES), lambda i: (0, 0)),
        ],
        out_specs=[pl.BlockSpec((tr, LANES), lambda i: (i, 0))] * 2,
        out_shape=[jax.ShapeDtypeStruct((t, LANES), F32)] * 2,
        compiler_params=_params(("arbitrary",)),
        name="rope_tables",
    )(positions.reshape(t, 1), invf, sign)


def _swa_kernel(sink_ref, q_ref, z_ref, kv_ref, cos_ref, sin_ref, o_ref, kprev_ref, vprev_ref,
                *, q_heads, kv_heads):
    blk = pl.program_id(1)
    tq = SWA_BLOCK
    hd = SWA_HEAD_DIM
    group = q_heads // kv_heads
    kvw = kv_heads * hd

    @pl.when(blk == 0)
    def _():
        kprev_ref[...] = jnp.zeros(kprev_ref.shape, kprev_ref.dtype)
        vprev_ref[...] = jnp.zeros(vprev_ref.shape, vprev_ref.dtype)

    cos = cos_ref[...]
    sin = sin_ref[...]
    lane = lax.broadcasted_iota(jnp.int32, (tq, LANES), 1)
    low_half = lane < hd
    first_quarter = (lane & (hd - 1)) < (hd // 2)

    def rope(xs):
        partner = jnp.where(first_quarter, pltpu.roll(xs, LANES - hd // 2, 1), pltpu.roll(xs, hd // 2, 1))
        return xs * cos + partner * sin

    def dup(xs, head_in_slab):
        swapped = pltpu.roll(xs, hd, 1)
        return jnp.where(low_half, xs, swapped) if head_in_slab == 0 else jnp.where(low_half, swapped, xs)

    qi = lax.broadcasted_iota(jnp.int32, (tq, 2 * tq), 0)
    kj = lax.broadcasted_iota(jnp.int32, (tq, 2 * tq), 1)
    lowest = jnp.maximum(qi, jnp.where(blk > 0, -1, tq - 1))
    valid = jnp.logical_and(kj > lowest, kj <= qi + tq)

    slabs_per_kv_pair = LANES // hd
    for g in range(kv_heads):
        slab = g // slabs_per_kv_pair
        pos_in_slab = g % slabs_per_kv_pair
        k_slab = rope(kv_ref[:, slab * LANES:(slab + 1) * LANES].astype(F32))
        v_slab = kv_ref[:, kvw + slab * LANES:kvw + (slab + 1) * LANES].astype(F32)
        k_cur = dup(k_slab, pos_in_slab).astype(BF16)
        v_cur = dup(v_slab, pos_in_slab).astype(BF16)
        k_all = jnp.concatenate([kprev_ref[g], k_cur], axis=0)
        v_all = jnp.concatenate([vprev_ref[g], v_cur], axis=0)

        q_parts = []
        for hh in range(group):
            head = g * group + hh
            qs = head // 2
            q_slab = rope(q_ref[:, qs * LANES:(qs + 1) * LANES].astype(F32)) * (hd ** -0.5)
            keep = low_half if head % 2 == 0 else jnp.logical_not(low_half)
            q_parts.append(jnp.where(keep, q_slab, 0.0).astype(BF16))
        scores = _mm_nt(jnp.concatenate(q_parts, axis=0), k_all)

        p_parts, inv_denoms = [], []
        for hh in range(group):
            sink = sink_ref[g * group + hh]
            s = jnp.where(valid, scores[hh * tq:(hh + 1) * tq], MASK_VALUE)
            m = jnp.maximum(jnp.max(s, axis=-1, keepdims=True), sink)
            p = jnp.exp(s - m)
            denom = jnp.sum(p, axis=-1, keepdims=True) + jnp.exp(sink - m)
            p_parts.append(p.astype(BF16))
            inv_denoms.append(1.0 / denom)
        pv = jnp.dot(jnp.concatenate(p_parts, axis=0), v_all, preferred_element_type=F32)

        for hh in range(0, group, 2):
            head = g * group + hh
            qs = head // 2
            o_even = pv[hh * tq:(hh + 1) * tq] * inv_denoms[hh]
            o_odd = pv[(hh + 1) * tq:(hh + 2) * tq] * inv_denoms[hh + 1]
            z = z_ref[:, qs * LANES:(qs + 1) * LANES].astype(F32)
            o_ref[:, qs * LANES:(qs + 1) * LANES] = (jnp.where(low_half, o_even, o_odd) * _silu(z)).astype(o_ref.dtype)

        kprev_ref[g] = k_cur
        vprev_ref[g] = v_cur


def _swa(u_main, kv, cos_t, sin_t, sinks, batch, q_heads, kv_heads, q_blk, z_blk):
    t = u_main.shape[0]
    w = q_heads * SWA_HEAD_DIM
    nb = t // batch // SWA_BLOCK

    def row(b, i):
        return b * nb + i

    return pl.pallas_call(
        functools.partial(_swa_kernel, q_heads=q_heads, kv_heads=kv_heads),
        grid=(batch, nb),
        in_specs=[
            pl.BlockSpec(memory_space=pltpu.SMEM),
            pl.BlockSpec((SWA_BLOCK, w), lambda b, i: (row(b, i), q_blk)),
            pl.BlockSpec((SWA_BLOCK, w), lambda b, i: (row(b, i), z_blk)),
            pl.BlockSpec((SWA_BLOCK, kv.shape[1]), lambda b, i: (row(b, i), 0)),
            pl.BlockSpec((SWA_BLOCK, LANES), lambda b, i: (row(b, i), 0)),
            pl.BlockSpec((SWA_BLOCK, LANES), lambda b, i: (row(b, i), 0)),
        ],
        out_specs=pl.BlockSpec((SWA_BLOCK, w), lambda b, i: (row(b, i), 0)),
        out_shape=jax.ShapeDtypeStruct((t, w), BF16),
        scratch_shapes=[
            pltpu.VMEM((kv_heads, SWA_BLOCK, LANES), BF16),
            pltpu.VMEM((kv_heads, SWA_BLOCK, LANES), BF16),
        ],
        compiler_params=_params(("arbitrary", "arbitrary")),
        name="sliding_window_attention",
    )(sinks, u_main, u_main, kv, cos_t, sin_t)


def _merge_kernel(oa_ref, ob_ref, ga_ref, gb_ref, pa_ref, pb_ref, y_ref):
    ya = jnp.dot(oa_ref[...], pa_ref[...], preferred_element_type=F32)
    yb = jnp.dot(ob_ref[...], pb_ref[...], preferred_element_type=F32)
    y = jax.nn.sigmoid(ga_ref[...].astype(F32)) * ya + jax.nn.sigmoid(gb_ref[...].astype(F32)) * yb
    y_ref[...] = y.astype(y_ref.dtype)


def _merge(o_a, o_b, u_main, proj_a, proj_b, ga_blk, gb_blk, tm):
    t, d = o_a.shape
    resident = dict(pipeline_mode=pl.Buffered(1))
    return pl.pallas_call(
        _merge_kernel,
        grid=(t // tm,),
        in_specs=[
            pl.BlockSpec((tm, d), lambda i: (i, 0)),
            pl.BlockSpec((tm, d), lambda i: (i, 0)),
            pl.BlockSpec((tm, d), lambda i: (i, ga_blk)),
            pl.BlockSpec((tm, d), lambda i: (i, gb_blk)),
            pl.BlockSpec(proj_a.shape, lambda i: (0, 0), **resident),
            pl.BlockSpec(proj_b.shape, lambda i: (0, 0), **resident),
        ],
        out_specs=pl.BlockSpec((tm, d), lambda i: (i, 0)),
        out_shape=jax.ShapeDtypeStruct((t, d), BF16),
        compiler_params=_params(("arbitrary",)),
        name="gated_merge",
    )(o_a, o_b, u_main, u_main, proj_a, proj_b)


def _out_kernel(y_ref, x_ref, w_ref, mod_ref, nmod_ref, nw_ref, *out_refs, final):
    r = jnp.dot(y_ref[...], w_ref[...], preferred_element_type=F32)
    x_new = x_ref[...] + mod_ref[0, 2:3, :] * r
    if final:
        (o_ref,) = out_refs
        o_ref[...] = x_new * lax.rsqrt(jnp.mean(x_new * x_new, axis=-1, keepdims=True) + NORM_EPS) * nw_ref[...]
    else:
        x_out_ref, h_ref = out_refs
        x_out_ref[...] = x_new
        h_ref[...] = _modulated_norm(x_new, nw_ref[...], nmod_ref[0]).astype(h_ref.dtype)


def _out_proj(y, x2, w_out, mod, next_mod, next_nw, batch, tm, final):
    t, d = x2.shape
    nt = t // batch // tm

    def row(b, i):
        return (b * nt + i, 0)

    if final:
        out_specs = pl.BlockSpec((tm, d), row)
        out_shape = jax.ShapeDtypeStruct((t, d), F32)
    else:
        out_specs = [pl.BlockSpec((tm, d), row)] * 2
        out_shape = [jax.ShapeDtypeStruct((t, d), F32), jax.ShapeDtypeStruct((t, d), BF16)]
    return pl.pallas_call(
        functools.partial(_out_kernel, final=final),
        grid=(batch, nt),
        in_specs=[
            pl.BlockSpec((tm, d), row),
            pl.BlockSpec((tm, d), row),
            pl.BlockSpec(w_out.shape, lambda b, i: (0, 0), pipeline_mode=pl.Buffered(1)),
            pl.BlockSpec((1, 3, d), lambda b, i: (b, 0, 0)),
            pl.BlockSpec((1, 3, d), lambda b, i: (b, 0, 0)),
            pl.BlockSpec((1, d), lambda b, i: (0, 0)),
        ],
        out_specs=out_specs,
        out_shape=out_shape,
        compiler_params=_params(("arbitrary", "arbitrary")),
        name="out_proj_final" if final else "out_proj",
    )(y, x2, w_out, mod, next_mod, next_nw)


def _tile(n, target, align):
    if n <= target:
        return n
    best = align
    for cand in range(align, target + 1, align):
        if n % cand == 0:
            best = cand
    assert n % best == 0
    return best


def kernel(x, c, positions, ada_w, ada_b, norm_w, w_in, conv_w, gdn_a_log, gdn_dt_bias, gdn_norm_w,
           swa_sinks, proj_a, proj_b, w_out, final_norm_w):
    batch, seq, d = x.shape
    depth = ada_w.shape[0]
    t = batch * seq
    gw = conv_w.shape[-1] // 3
    gh = gdn_a_log.shape[-1]
    sw = proj_b.shape[1]
    q_heads = swa_sinks.shape[-1]
    in_cols = w_in.shape[-1]
    kvw = (in_cols - 4 * gw - 2 * gh - 2 * sw - 2 * d) // 2
    kv_heads = kvw // SWA_HEAD_DIM
    assert gw == d and sw == d, "column-block layout assumes both branch widths equal d_model"
    assert gw == gh * GDN_HEAD_DIM and sw == q_heads * SWA_HEAD_DIM
    assert 2 * gh <= LANES and kvw % LANES == 0 and q_heads % kv_heads == 0
    assert seq % SWA_BLOCK == 0 and seq % GDN_CHUNK == 0

    sizes = (3 * gw, gh, gh, gw, sw, kvw, kvw, sw, d, d)
    offs = [0]
    for s in sizes:
        offs.append(offs[-1] + s)
    seg = [w_in[:, :, offs[i]:offs[i + 1]] for i in range(len(sizes))]
    qkv_a, beta_a, a_a, z_a, q_b, k_b, v_b, z_b, g_a, g_b = seg
    w_main = jnp.concatenate([qkv_a, z_a, q_b, z_b, g_a, g_b], axis=-1).astype(BF16)
    pad = jnp.zeros((depth, d, LANES - 2 * gh), w_in.dtype)
    w_tail = jnp.concatenate([k_b, v_b, beta_a, a_a, pad], axis=-1).astype(BF16)
    blk_qa, blk_za, blk_qb, blk_zb, blk_ga, blk_gb = 0, 3, 4, 5, 6, 7
    proj_a16, proj_b16, w_out16 = proj_a.astype(BF16), proj_b.astype(BF16), w_out.astype(BF16)

    head_params = jnp.zeros((depth, 8, LANES), F32)
    head_params = head_params.at[:, 0, gh:2 * gh].set(gdn_a_log.astype(F32))
    head_params = head_params.at[:, 1, gh:2 * gh].set(gdn_dt_bias.astype(F32))

    tm_big = _tile(seq, 1024, 8)
    tm_mid = _tile(seq, 256, 8)
    mod = _modulation(c, ada_w, ada_b, _tile(3 * d, 1024, LANES)).reshape(depth, batch, 3, d)
    cos_t, sin_t = _rope_tables(positions, _tile(t, 512, 8))

    x2 = x.reshape(t, d)
    h = _prep(x2, mod[0], norm_w[0:1], batch, _tile(seq, 512, 8))
    out = None
    for l in range(depth):
        u_main = _in_proj_main(h, w_main[l], tm_big, _tile(8 * d, 1024, LANES))
        kv, gb = _in_proj_tail(h, w_tail[l], 2 * kvw, tm_big)
        o_a = _gdn(u_main, gb, conv_w[l], head_params[l], gdn_norm_w[l:l + 1], batch, gh, GDN_CHUNK)
        o_b = _swa(u_main, kv, cos_t, sin_t, swa_sinks[l], batch, q_heads, kv_heads, blk_qb, blk_zb)
        y = _merge(o_a, o_b, u_main, proj_a16[l], proj_b16[l], blk_ga, blk_gb, tm_mid)
        final = l == depth - 1
        if final:
            out = _out_proj(y, x2, w_out16[l], mod[l], mod[l], final_norm_w.reshape(1, d), batch, tm_mid, True)
        else:
            x2, h = _out_proj(y, x2, w_out16[l], mod[l], mod[l + 1], norm_w[l + 1:l + 2], batch, tm_mid, False)
    return out.reshape(batch, seq, d)
```

```python
import functools

import jax
import jax.numpy as jnp
from jax import lax
from jax.experimental import pallas as pl
from jax.experimental.pallas import tpu as pltpu

F32 = jnp.float32
BF16 = jnp.bfloat16

LANES = 128
GDN_HEAD_DIM = 128
GDN_CHUNK = 64
CONV_K = 4
CONV_HALO = 8
SWA_HEAD_DIM = 64
SWA_BLOCK = 128
ROPE_THETA = 10000.0
NORM_EPS = 1e-6
L2_EPS = 1e-6
MASK_VALUE = -1e30
VMEM_LIMIT = 56 * 1024 * 1024


def _params(semantics):
    return pltpu.CompilerParams(dimension_semantics=semantics, vmem_limit_bytes=VMEM_LIMIT)


def _mm(a, b):
    return jnp.dot(a.astype(BF16), b.astype(BF16), preferred_element_type=F32)


def _mm_nt(a, b):
    return lax.dot_general(a.astype(BF16), b.astype(BF16), (((1,), (1,)), ((), ())),
                           preferred_element_type=F32)


def _mm_tn(a, b):
    return lax.dot_general(a.astype(BF16), b.astype(BF16), (((0,), (0,)), ((), ())),
                           preferred_element_type=F32)


def _silu(x):
    return x * jax.nn.sigmoid(x)


def _softplus(x):
    return jnp.maximum(x, 0.0) + jnp.log1p(jnp.exp(-jnp.abs(x)))


def _mod_kernel(c_ref, w_ref, b_ref, o_ref):
    c = c_ref[...]
    act = _silu(c)
    hi = act.astype(BF16)
    lo = (act - hi.astype(F32)).astype(BF16)
    w = w_ref[0].astype(BF16)
    acc = jnp.dot(hi, w, preferred_element_type=F32) + jnp.dot(lo, w, preferred_element_type=F32)
    o_ref[0] = acc + b_ref[0]


def _modulation(c, ada_w, ada_b, tn):
    depth, d, n = ada_w.shape
    b = c.shape[0]
    return pl.pallas_call(
        _mod_kernel,
        grid=(depth, n // tn),
        in_specs=[
            pl.BlockSpec((b, d), lambda l, j: (0, 0)),
            pl.BlockSpec((1, d, tn), lambda l, j: (l, 0, j)),
            pl.BlockSpec((1, 1, tn), lambda l, j: (l, 0, j)),
        ],
        out_specs=pl.BlockSpec((1, b, tn), lambda l, j: (l, 0, j)),
        out_shape=jax.ShapeDtypeStruct((depth, b, n), F32),
        compiler_params=_params(("arbitrary", "arbitrary")),
        name="adaln_modulation",
    )(c, ada_w, ada_b.reshape(depth, 1, n))


def _modulated_norm(x, nw, mod):
    y = x * lax.rsqrt(jnp.mean(x * x, axis=-1, keepdims=True) + NORM_EPS) * nw
    return y * (1.0 + mod[1:2, :]) + mod[0:1, :]


def _prep_kernel(x_ref, mod_ref, nw_ref, h_ref):
    h_ref[...] = _modulated_norm(x_ref[...], nw_ref[...], mod_ref[0]).astype(h_ref.dtype)


def _prep(x2, mod, nw, batch, tm):
    t, d = x2.shape
    nt = t // batch // tm
    return pl.pallas_call(
        _prep_kernel,
        grid=(batch, nt),
        in_specs=[
            pl.BlockSpec((tm, d), lambda b, i: (b * nt + i, 0)),
            pl.BlockSpec((1, 3, d), lambda b, i: (b, 0, 0)),
            pl.BlockSpec((1, d), lambda b, i: (0, 0)),
        ],
        out_specs=pl.BlockSpec((tm, d), lambda b, i: (b * nt + i, 0)),
        out_shape=jax.ShapeDtypeStruct((t, d), BF16),
        compiler_params=_params(("arbitrary", "arbitrary")),
        name="modulated_norm",
    )(x2, mod, nw)


def _matmul_kernel(a_ref, w_ref, o_ref):
    o_ref[...] = jnp.dot(a_ref[...], w_ref[...], preferred_element_type=F32).astype(o_ref.dtype)


def _in_proj_main(h, w, tm, tn):
    t, d = h.shape
    n = w.shape[1]
    return pl.pallas_call(
        _matmul_kernel,
        grid=(n // tn, t // tm),
        in_specs=[
            pl.BlockSpec((tm, d), lambda j, i: (i, 0)),
            pl.BlockSpec((d, tn), lambda j, i: (0, j)),
        ],
        out_specs=pl.BlockSpec((tm, tn), lambda j, i: (i, j)),
        out_shape=jax.ShapeDtypeStruct((t, n), BF16),
        compiler_params=_params(("arbitrary", "arbitrary")),
        name="in_proj_main",
    )(h, w)


def _tail_kernel(a_ref, w_ref, kv_ref, gb_ref, *, kv_cols):
    r = jnp.dot(a_ref[...], w_ref[...], preferred_element_type=F32)
    kv_ref[...] = r[:, :kv_cols].astype(kv_ref.dtype)
    gb_ref[...] = r[:, kv_cols:]


def _in_proj_tail(h, w, kv_cols, tm):
    t, d = h.shape
    n = w.shape[1]
    return pl.pallas_call(
        functools.partial(_tail_kernel, kv_cols=kv_cols),
        grid=(t // tm,),
        in_specs=[
            pl.BlockSpec((tm, d), lambda i: (i, 0)),
            pl.BlockSpec((d, n), lambda i: (0, 0)),
        ],
        out_specs=[
            pl.BlockSpec((tm, kv_cols), lambda i: (i, 0)),
            pl.BlockSpec((tm, n - kv_cols), lambda i: (i, 0)),
        ],
        out_shape=[
            jax.ShapeDtypeStruct((t, kv_cols), BF16),
            jax.ShapeDtypeStruct((t, n - kv_cols), F32),
        ],
        compiler_params=_params(("arbitrary",)),
        name="in_proj_tail",
    )(h, w)


INV_BASE = 8


def _inverse_masks(n):
    ii = lax.broadcasted_iota(jnp.int32, (n, n), 0)
    jj = lax.broadcasted_iota(jnp.int32, (n, n), 1)
    shift = INV_BASE.bit_length() - 1
    base = (ii >> shift) == (jj >> shift)
    levels = []
    blk = INV_BASE
    while blk < n:
        same_parent = (ii >> (shift + 1)) == (jj >> (shift + 1))
        levels.append(jnp.logical_and(same_parent, (ii >> shift) != (jj >> shift)))
        shift += 1
        blk *= 2
    return base, levels


def _unit_lower_inverse(lms, eye, masks):
    base, levels = masks
    ps = [jnp.where(base, lm, 0.0) for lm in lms]
    xs = [eye - p for p in ps]
    step = 2
    while step < INV_BASE:
        ps = [_mm(p, p) for p in ps]
        xs = [x + _mm(x, p) for x, p in zip(xs, ps)]
        step *= 2
    for off_mask in levels:
        ts = [_mm(x, jnp.where(off_mask, lm, 0.0)) for x, lm in zip(xs, lms)]
        xs = [x - _mm(t, x) for x, t in zip(xs, ts)]
    return xs


def _chunk_cumsum(g):
    rows = lax.broadcasted_iota(jnp.int32, g.shape, 0)
    s = 1
    while s < g.shape[0]:
        g = g + jnp.where(rows >= s, pltpu.roll(g, s, 0), 0.0)
        s *= 2
    return g


def _gdn_kernel(q_ref, k_ref, v_ref, z_ref, gb_ref, cw_ref, hp_ref, nw_ref, o_ref,
                xpad_ref, state_ref, *, ts, heads):
    c = GDN_CHUNK
    hd = GDN_HEAD_DIM
    w = heads * hd
    t = pl.program_id(1)

    @pl.when(t == 0)
    def _():
        xpad_ref[0:CONV_HALO, :] = jnp.zeros((CONV_HALO, 3 * w), F32)
        state_ref[...] = jnp.zeros(state_ref.shape, F32)

    @pl.when(t > 0)
    def _():
        xpad_ref[0:CONV_HALO, :] = xpad_ref[ts:ts + CONV_HALO, :]

    xpad_ref[CONV_HALO:CONV_HALO + ts, 0:w] = q_ref[...].astype(F32)
    xpad_ref[CONV_HALO:CONV_HALO + ts, w:2 * w] = k_ref[...].astype(F32)
    xpad_ref[CONV_HALO:CONV_HALO + ts, 2 * w:3 * w] = v_ref[...].astype(F32)

    ii = lax.broadcasted_iota(jnp.int32, (c, c), 0)
    jj = lax.broadcasted_iota(jnp.int32, (c, c), 1)
    causal = ii >= jj
    strict = ii > jj
    eye = jnp.where(ii == jj, 1.0, 0.0).astype(F32)
    inv_masks = _inverse_masks(c)
    norm_w = nw_ref[...]

    for ci in range(ts // c):
        r0 = ci * c
        gb = gb_ref[r0:r0 + c, :]
        beta_all = jax.nn.sigmoid(gb)
        g_all = -jnp.exp(hp_ref[0:1, :]) * _softplus(gb + hp_ref[1:2, :])
        gc_all = _chunk_cumsum(g_all)
        gc_t = jnp.concatenate([gc_all, jnp.zeros((LANES - c, LANES), F32)], axis=0).T
        glast = gc_all[c - 1:c, :]
        egc_all = jnp.exp(gc_all)
        ekd_all = jnp.exp(glast - gc_all)
        cdec_all = jnp.exp(glast)

        def conv_silu(part, h):
            col = part * w + h * hd
            acc = None
            for j in range(CONV_K):
                start = r0 + CONV_HALO - (CONV_K - 1) + j
                term = xpad_ref[start:start + c, col:col + hd] * cw_ref[j:j + 1, col:col + hd]
                acc = term if acc is None else acc + term
            return _silu(acc)

        hs = range(heads)
        q = [conv_silu(0, h) for h in hs]
        k = [conv_silu(1, h) for h in hs]
        v = [conv_silu(2, h) for h in hs]
        q = [x * lax.rsqrt(jnp.sum(x * x, axis=-1, keepdims=True) + L2_EPS) * (hd ** -0.5) for x in q]
        k = [x * lax.rsqrt(jnp.sum(x * x, axis=-1, keepdims=True) + L2_EPS) for x in k]

        bcol = [beta_all[:, h:h + 1] for h in hs]
        gcol = [gc_all[:, heads + h:heads + h + 1] for h in hs]
        egcol = [egc_all[:, heads + h:heads + h + 1] for h in hs]
        ekdcol = [ekd_all[:, heads + h:heads + h + 1] for h in hs]
        cdec = [cdec_all[:, heads + h:heads + h + 1] for h in hs]
        grow = [gc_t[heads + h:heads + h + 1, 0:c] for h in hs]

        decay = [jnp.where(causal, jnp.exp(jnp.where(causal, gcol[h] - grow[h], 0.0)), 0.0) for h in hs]
        kb = [k[h] * bcol[h] for h in hs]
        both = [_mm_nt(jnp.concatenate([kb[h], q[h]], axis=0), k[h]) for h in hs]
        lm = [jnp.where(strict, both[h][0:c] * decay[h], 0.0) for h in hs]
        qkm = [both[h][c:2 * c] * decay[h] for h in hs]
        tinv = _unit_lower_inverse(lm, eye, inv_masks)
        sol = [_mm(tinv[h], jnp.concatenate([v[h] * bcol[h], kb[h] * egcol[h]], axis=1)) for h in hs]

        s_old = [state_ref[h] for h in hs]
        v_new = [sol[h][:, 0:hd] - _mm(sol[h][:, hd:2 * hd], s_old[h]) for h in hs]
        o_inter = [_mm(q[h] * egcol[h], s_old[h]) for h in hs]
        o = [o_inter[h] + _mm(qkm[h], v_new[h]) for h in hs]
        for h in hs:
            state_ref[h] = s_old[h] * cdec[h] + _mm_tn(k[h] * ekdcol[h], v_new[h])

        for h in hs:
            on = o[h] * lax.rsqrt(jnp.mean(o[h] * o[h], axis=-1, keepdims=True) + NORM_EPS) * norm_w
            z = z_ref[r0:r0 + c, h * hd:(h + 1) * hd].astype(F32)
            o_ref[r0:r0 + c, h * hd:(h + 1) * hd] = (on * _silu(z)).astype(o_ref.dtype)


def _gdn(u_main, gb, conv_w, head_params, norm_w, batch, heads, ts):
    t = u_main.shape[0]
    w = heads * GDN_HEAD_DIM
    nt = t // batch // ts

    def col(blk):
        return pl.BlockSpec((ts, w), lambda b, i, blk=blk: (b * nt + i, blk))

    return pl.pallas_call(
        functools.partial(_gdn_kernel, ts=ts, heads=heads),
        grid=(batch, nt),
        in_specs=[
            col(0), col(1), col(2), col(3),
            pl.BlockSpec((ts, LANES), lambda b, i: (b * nt + i, 0)),
            pl.BlockSpec((CONV_K, 3 * w), lambda b, i: (0, 0)),
            pl.BlockSpec((8, LANES), lambda b, i: (0, 0)),
            pl.BlockSpec((1, GDN_HEAD_DIM), lambda b, i: (0, 0)),
        ],
        out_specs=pl.BlockSpec((ts, w), lambda b, i: (b * nt + i, 0)),
        out_shape=jax.ShapeDtypeStruct((t, w), BF16),
        scratch_shapes=[
            pltpu.VMEM((ts + CONV_HALO, 3 * w), F32),
            pltpu.VMEM((heads, GDN_HEAD_DIM, GDN_HEAD_DIM), F32),
        ],
        compiler_params=_params(("arbitrary", "arbitrary")),
        name="gated_deltanet",
    )(u_main, u_main, u_main, u_main, gb, conv_w, head_params, norm_w)


def _rope_table_kernel(pos_ref, invf_ref, sign_ref, cos_ref, sin_ref):
    ang = pos_ref[...].astype(F32) * invf_ref[...]
    cos_ref[...] = jnp.cos(ang)
    sin_ref[...] = jnp.sin(ang) * sign_ref[...]


def _rope_tables(positions, tr):
    t = positions.size
    half = SWA_HEAD_DIM // 2
    inv_freq = ROPE_THETA ** (-jnp.arange(half, dtype=F32) / half)
    invf = jnp.tile(inv_freq, LANES // half).reshape(1, LANES)
    sign = jnp.tile(jnp.concatenate([-jnp.ones((half,), F32), jnp.ones((half,), F32)]),
                    LANES // SWA_HEAD_DIM).reshape(1, LANES)
    return pl.pallas_call(
        _rope_table_kernel,
        grid=(t // tr,),
        in_specs=[
            pl.BlockSpec((tr, 1), lambda i: (i, 0)),
            pl.BlockSpec((1, LANES), lambda i: (0, 0)),
            pl.BlockSpec((1, LANES), lambda i: (0, 0)),
        ],
        out_specs=[pl.BlockSpec((tr, LANES), lambda i: (i, 0))] * 2,
        out_shape=[jax.ShapeDtypeStruct((t, LANES), F32)] * 2,
        compiler_params=_params(("arbitrary",)),
        name="rope_tables",
    )(positions.reshape(t, 1), invf, sign)


def _swa_kernel(sink_ref, q_ref, z_ref, kv_ref, cos_ref, sin_ref, o_ref, kprev_ref, vprev_ref,
                *, q_heads, kv_heads):
    blk = pl.program_id(1)
    tq = SWA_BLOCK
    hd = SWA_HEAD_DIM
    group = q_heads // kv_heads
    kvw = kv_heads * hd

    @pl.when(blk == 0)
    def _():
        kprev_ref[...] = jnp.zeros(kprev_ref.shape, kprev_ref.dtype)
        vprev_ref[...] = jnp.zeros(vprev_ref.shape, vprev_ref.dtype)

    cos = cos_ref[...]
    sin = sin_ref[...]
    lane = lax.broadcasted_iota(jnp.int32, (tq, LANES), 1)
    low_half = lane < hd
    first_quarter = (lane & (hd - 1)) < (hd // 2)

    def rope(xs):
        partner = jnp.where(first_quarter, pltpu.roll(xs, LANES - hd // 2, 1), pltpu.roll(xs, hd // 2, 1))
        return xs * cos + partner * sin

    def dup(xs, head_in_slab):
        swapped = pltpu.roll(xs, hd, 1)
        return jnp.where(low_half, xs, swapped) if head_in_slab == 0 else jnp.where(low_half, swapped, xs)

    qi = lax.broadcasted_iota(jnp.int32, (tq, 2 * tq), 0)
    kj = lax.broadcasted_iota(jnp.int32, (tq, 2 * tq), 1)
    lowest = jnp.maximum(qi, jnp.where(blk > 0, -1, tq - 1))
    valid = jnp.logical_and(kj > lowest, kj <= qi + tq)

    slabs_per_kv_pair = LANES // hd
    for g in range(kv_heads):
        slab = g // slabs_per_kv_pair
        pos_in_slab = g % slabs_per_kv_pair
        k_slab = rope(kv_ref[:, slab * LANES:(slab + 1) * LANES].astype(F32))
        v_slab = kv_ref[:, kvw + slab * LANES:kvw + (slab + 1) * LANES].astype(F32)
        k_cur = dup(k_slab, pos_in_slab).astype(BF16)
        v_cur = dup(v_slab, pos_in_slab).astype(BF16)
        k_all = jnp.concatenate([kprev_ref[g], k_cur], axis=0)
        v_all = jnp.concatenate([vprev_ref[g], v_cur], axis=0)

        q_parts = []
        for hh in range(group):
            head = g * group + hh
            qs = head // 2
            q_slab = rope(q_ref[:, qs * LANES:(qs + 1) * LANES].astype(F32)) * (hd ** -0.5)
            keep = low_half if head % 2 == 0 else jnp.logical_not(low_half)
            q_parts.append(jnp.where(keep, q_slab, 0.0).astype(BF16))
        scores = _mm_nt(jnp.concatenate(q_parts, axis=0), k_all)

        p_parts, inv_denoms = [], []
        for hh in range(group):
            sink = sink_ref[g * group + hh]
            s = jnp.where(valid, scores[hh * tq:(hh + 1) * tq], MASK_VALUE)
            m = jnp.maximum(jnp.max(s, axis=-1, keepdims=True), sink)
            p = jnp.exp(s - m)
            denom = jnp.sum(p, axis=-1, keepdims=True) + jnp.exp(sink - m)
            p_parts.append(p.astype(BF16))
            inv_denoms.append(1.0 / denom)
        pv = jnp.dot(jnp.concatenate(p_parts, axis=0), v_all, preferred_element_type=F32)

        for hh in range(0, group, 2):
            head = g * group + hh
            qs = head // 2
            o_even = pv[hh * tq:(hh + 1) * tq] * inv_denoms[hh]
            o_odd = pv[(hh + 1) * tq:(hh + 2) * tq] * inv_denoms[hh + 1]
            z = z_ref[:, qs * LANES:(qs + 1) * LANES].astype(F32)
            o_ref[:, qs * LANES:(qs + 1) * LANES] = (jnp.where(low_half, o_even, o_odd) * _silu(z)).astype(o_ref.dtype)

        kprev_ref[g] = k_cur
        vprev_ref[g] = v_cur


def _swa(u_main, kv, cos_t, sin_t, sinks, batch, q_heads, kv_heads, q_blk, z_blk):
    t = u_main.shape[0]
    w = q_heads * SWA_HEAD_DIM
    nb = t // batch // SWA_BLOCK

    def row(b, i):
        return b * nb + i

    return pl.pallas_call(
        functools.partial(_swa_kernel, q_heads=q_heads, kv_heads=kv_heads),
        grid=(batch, nb),
        in_specs=[
            pl.BlockSpec(memory_space=pltpu.SMEM),
            pl.BlockSpec((SWA_BLOCK, w), lambda b, i: (row(b, i), q_blk)),
            pl.BlockSpec((SWA_BLOCK, w), lambda b, i: (row(b, i), z_blk)),
            pl.BlockSpec((SWA_BLOCK, kv.shape[1]), lambda b, i: (row(b, i), 0)),
            pl.BlockSpec((SWA_BLOCK, LANES), lambda b, i: (row(b, i), 0)),
            pl.BlockSpec((SWA_BLOCK, LANES), lambda b, i: (row(b, i), 0)),
        ],
        out_specs=pl.BlockSpec((SWA_BLOCK, w), lambda b, i: (row(b, i), 0)),
        out_shape=jax.ShapeDtypeStruct((t, w), BF16),
        scratch_shapes=[
            pltpu.VMEM((kv_heads, SWA_BLOCK, LANES), BF16),
            pltpu.VMEM((kv_heads, SWA_BLOCK, LANES), BF16),
        ],
        compiler_params=_params(("arbitrary", "arbitrary")),
        name="sliding_window_attention",
    )(sinks, u_main, u_main, kv, cos_t, sin_t)


def _merge_kernel(oa_ref, ob_ref, ga_ref, gb_ref, pa_ref, pb_ref, y_ref):
    ya = jnp.dot(oa_ref[...], pa_ref[...], preferred_element_type=F32)
    yb = jnp.dot(ob_ref[...], pb_ref[...], preferred_element_type=F32)
    y = jax.nn.sigmoid(ga_ref[...].astype(F32)) * ya + jax.nn.sigmoid(gb_ref[...].astype(F32)) * yb
    y_ref[...] = y.astype(y_ref.dtype)


def _merge(o_a, o_b, u_main, proj_a, proj_b, ga_blk, gb_blk, tm):
    t, d = o_a.shape
    resident = dict(pipeline_mode=pl.Buffered(1))
    return pl.pallas_call(
        _merge_kernel,
        grid=(t // tm,),
        in_specs=[
            pl.BlockSpec((tm, d), lambda i: (i, 0)),
            pl.BlockSpec((tm, d), lambda i: (i, 0)),
            pl.BlockSpec((tm, d), lambda i: (i, ga_blk)),
            pl.BlockSpec((tm, d), lambda i: (i, gb_blk)),
            pl.BlockSpec(proj_a.shape, lambda i: (0, 0), **resident),
            pl.BlockSpec(proj_b.shape, lambda i: (0, 0), **resident),
        ],
        out_specs=pl.BlockSpec((tm, d), lambda i: (i, 0)),
        out_shape=jax.ShapeDtypeStruct((t, d), BF16),
        compiler_params=_params(("arbitrary",)),
        name="gated_merge",
    )(o_a, o_b, u_main, u_main, proj_a, proj_b)


def _out_kernel(y_ref, x_ref, w_ref, mod_ref, nmod_ref, nw_ref, *out_refs, final):
    r = jnp.dot(y_ref[...], w_ref[...], preferred_element_type=F32)
    x_new = x_ref[...] + mod_ref[0, 2:3, :] * r
    if final:
        (o_ref,) = out_refs
        o_ref[...] = x_new * lax.rsqrt(jnp.mean(x_new * x_new, axis=-1, keepdims=True) + NORM_EPS) * nw_ref[...]
    else:
        x_out_ref, h_ref = out_refs
        x_out_ref[...] = x_new
        h_ref[...] = _modulated_norm(x_new, nw_ref[...], nmod_ref[0]).astype(h_ref.dtype)


def _out_proj(y, x2, w_out, mod, next_mod, next_nw, batch, tm, final):
    t, d = x2.shape
    nt = t // batch // tm

    def row(b, i):
        return (b * nt + i, 0)

    if final:
        out_specs = pl.BlockSpec((tm, d), row)
        out_shape = jax.ShapeDtypeStruct((t, d), F32)
    else:
        out_specs = [pl.BlockSpec((tm, d), row)] * 2
        out_shape = [jax.ShapeDtypeStruct((t, d), F32), jax.ShapeDtypeStruct((t, d), BF16)]
    return pl.pallas_call(
        functools.partial(_out_kernel, final=final),
        grid=(batch, nt),
        in_specs=[
            pl.BlockSpec((tm, d), row),
            pl.BlockSpec((tm, d), row),
            pl.BlockSpec(w_out.shape, lambda b, i: (0, 0), pipeline_mode=pl.Buffered(1)),
            pl.BlockSpec((1, 3, d), lambda b, i: (b, 0, 0)),
            pl.BlockSpec((1, 3, d), lambda b, i: (b, 0, 0)),
            pl.BlockSpec((1, d), lambda b, i: (0, 0)),
        ],
        out_specs=out_specs,
        out_shape=out_shape,
        compiler_params=_params(("arbitrary", "arbitrary")),
        name="out_proj_final" if final else "out_proj",
    )(y, x2, w_out, mod, next_mod, next_nw)


def _tile(n, target, align):
    if n <= target:
        return n
    best = align
    for cand in range(align, target + 1, align):
        if n % cand == 0:
            best = cand
    assert n % best == 0
    return best


def kernel(x, c, positions, ada_w, ada_b, norm_w, w_in, conv_w, gdn_a_log, gdn_dt_bias, gdn_norm_w,
           swa_sinks, proj_a, proj_b, w_out, final_norm_w):
    batch, seq, d = x.shape
    depth = ada_w.shape[0]
    t = batch * seq
    gw = conv_w.shape[-1] // 3
    gh = gdn_a_log.shape[-1]
    sw = proj_b.shape[1]
    q_heads = swa_sinks.shape[-1]
    in_cols = w_in.shape[-1]
    kvw = (in_cols - 4 * gw - 2 * gh - 2 * sw - 2 * d) // 2
    kv_heads = kvw // SWA_HEAD_DIM
    assert gw == d and sw == d, "column-block layout assumes both branch widths equal d_model"
    assert gw == gh * GDN_HEAD_DIM and sw == q_heads * SWA_HEAD_DIM
    assert 2 * gh <= LANES and kvw % LANES == 0 and q_heads % kv_heads == 0
    assert seq % SWA_BLOCK == 0 and seq % GDN_CHUNK == 0

    sizes = (3 * gw, gh, gh, gw, sw, kvw, kvw, sw, d, d)
    offs = [0]
    for s in sizes:
        offs.append(offs[-1] + s)
    seg = [w_in[:, :, offs[i]:offs[i + 1]] for i in range(len(sizes))]
    qkv_a, beta_a, a_a, z_a, q_b, k_b, v_b, z_b, g_a, g_b = seg
    w_main = jnp.concatenate([qkv_a, z_a, q_b, z_b, g_a, g_b], axis=-1).astype(BF16)
    pad = jnp.zeros((depth, d, LANES - 2 * gh), w_in.dtype)
    w_tail = jnp.concatenate([k_b, v_b, beta_a, a_a, pad], axis=-1).astype(BF16)
    blk_qa, blk_za, blk_qb, blk_zb, blk_ga, blk_gb = 0, 3, 4, 5, 6, 7
    proj_a16, proj_b16, w_out16 = proj_a.astype(BF16), proj_b.astype(BF16), w_out.astype(BF16)

    head_params = jnp.zeros((depth, 8, LANES), F32)
    head_params = head_params.at[:, 0, gh:2 * gh].set(gdn_a_log.astype(F32))
    head_params = head_params.at[:, 1, gh:2 * gh].set(gdn_dt_bias.astype(F32))

    tm_big = _tile(seq, 1024, 8)
    tm_mid = _tile(seq, 256, 8)
    mod = _modulation(c, ada_w, ada_b, _tile(3 * d, 1024, LANES)).reshape(depth, batch, 3, d)
    cos_t, sin_t = _rope_tables(positions, _tile(t, 512, 8))

    x2 = x.reshape(t, d)
    h = _prep(x2, mod[0], norm_w[0:1], batch, _tile(seq, 512, 8))
    out = None
    for l in range(depth):
        u_main = _in_proj_main(h, w_main[l], tm_big, _tile(8 * d, 1024, LANES))
        kv, gb = _in_proj_tail(h, w_tail[l], 2 * kvw, tm_big)
        o_a = _gdn(u_main, gb, conv_w[l], head_params[l], gdn_norm_w[l:l + 1], batch, gh, GDN_CHUNK)
        o_b = _swa(u_main, kv, cos_t, sin_t, swa_sinks[l], batch, q_heads, kv_heads, blk_qb, blk_zb)
        y = _merge(o_a, o_b, u_main, proj_a16[l], proj_b16[l], blk_ga, blk_gb, tm_mid)
        final = l == depth - 1
        if final:
            out = _out_proj(y, x2, w_out16[l], mod[l], mod[l], final_norm_w.reshape(1, d), batch, tm_mid, True)
        else:
            x2, h = _out_proj(y, x2, w_out16[l], mod[l], mod[l + 1], norm_w[l + 1:l + 2], batch, tm_mid, False)
    return out.reshape(batch, seq, d)
```

```python
import functools

import jax
import jax.numpy as jnp
from jax import lax
from jax.experimental import pallas as pl
from jax.experimental.pallas import tpu as pltpu

F32 = jnp.float32
BF16 = jnp.bfloat16

LANES = 128
GDN_HEAD_DIM = 128
GDN_CHUNK = 64
CONV_K = 4
CONV_HALO = 16
SWA_HEAD_DIM = 64
SWA_BLOCK = 128
ROPE_THETA = 10000.0
NORM_EPS = 1e-6
L2_EPS = 1e-6
MASK_VALUE = -1e30
LOG2_E = 1.4426950408889634
VMEM_LIMIT = 56 * 1024 * 1024


def _params(semantics):
    return pltpu.CompilerParams(dimension_semantics=semantics, vmem_limit_bytes=VMEM_LIMIT)


def _mm(a, b):
    return jnp.dot(a.astype(BF16), b.astype(BF16), preferred_element_type=F32)


def _mm_nt(a, b):
    return lax.dot_general(a.astype(BF16), b.astype(BF16), (((1,), (1,)), ((), ())),
                           preferred_element_type=F32)


def _mm_tn(a, b):
    return lax.dot_general(a.astype(BF16), b.astype(BF16), (((0,), (0,)), ((), ())),
                           preferred_element_type=F32)


def _silu(x):
    return x * jax.nn.sigmoid(x)


def _softplus(x):
    return jnp.maximum(x, 0.0) + jnp.log1p(jnp.exp(-jnp.abs(x)))


def _mod_kernel(c_ref, w_ref, b_ref, o_ref):
    c = c_ref[...]
    act = _silu(c)
    hi = act.astype(BF16)
    lo = (act - hi.astype(F32)).astype(BF16)
    w = w_ref[0].astype(BF16)
    acc = jnp.dot(hi, w, preferred_element_type=F32) + jnp.dot(lo, w, preferred_element_type=F32)
    o_ref[0] = acc + b_ref[0]


def _modulation(c, ada_w, ada_b, tn):
    depth, d, n = ada_w.shape
    b = c.shape[0]
    return pl.pallas_call(
        _mod_kernel,
        grid=(depth, n // tn),
        in_specs=[
            pl.BlockSpec((b, d), lambda l, j: (0, 0)),
            pl.BlockSpec((1, d, tn), lambda l, j: (l, 0, j)),
            pl.BlockSpec((1, 1, tn), lambda l, j: (l, 0, j)),
        ],
        out_specs=pl.BlockSpec((1, b, tn), lambda l, j: (l, 0, j)),
        out_shape=jax.ShapeDtypeStruct((depth, b, n), F32),
        compiler_params=_params(("arbitrary", "arbitrary")),
        name="adaln_modulation",
    )(c, ada_w, ada_b.reshape(depth, 1, n))


def _modulated_norm(x, nw, mod):
    y = x * lax.rsqrt(jnp.mean(x * x, axis=-1, keepdims=True) + NORM_EPS) * nw
    return y * (1.0 + mod[1:2, :]) + mod[0:1, :]


def _prep_kernel(x_ref, mod_ref, nw_ref, h_ref):
    h_ref[...] = _modulated_norm(x_ref[...], nw_ref[...], mod_ref[0]).astype(h_ref.dtype)


def _prep(x2, mod, nw, batch, tm):
    t, d = x2.shape
    nt = t // batch // tm
    return pl.pallas_call(
        _prep_kernel,
        grid=(batch, nt),
        in_specs=[
            pl.BlockSpec((tm, d), lambda b, i: (b * nt + i, 0)),
            pl.BlockSpec((1, 3, d), lambda b, i: (b, 0, 0)),
            pl.BlockSpec((1, d), lambda b, i: (0, 0)),
        ],
        out_specs=pl.BlockSpec((tm, d), lambda b, i: (b * nt + i, 0)),
        out_shape=jax.ShapeDtypeStruct((t, d), BF16),
        compiler_params=_params(("arbitrary", "arbitrary")),
        name="modulated_norm",
    )(x2, mod, nw)


def _matmul_kernel(a_ref, w_ref, o_ref):
    o_ref[...] = jnp.dot(a_ref[...], w_ref[...], preferred_element_type=F32).astype(o_ref.dtype)


def _in_proj_main(h, w, tm, tn):
    t, d = h.shape
    n = w.shape[1]
    return pl.pallas_call(
        _matmul_kernel,
        grid=(n // tn, t // tm),
        in_specs=[
            pl.BlockSpec((tm, d), lambda j, i: (i, 0)),
            pl.BlockSpec((d, tn), lambda j, i: (0, j)),
        ],
        out_specs=pl.BlockSpec((tm, tn), lambda j, i: (i, j)),
        out_shape=jax.ShapeDtypeStruct((t, n), BF16),
        compiler_params=_params(("arbitrary", "arbitrary")),
        name="in_proj_main",
    )(h, w)


def _tail_kernel(a_ref, w_ref, kv_ref, gb_ref, *, kv_cols):
    r = jnp.dot(a_ref[...], w_ref[...], preferred_element_type=F32)
    kv_ref[...] = r[:, :kv_cols].astype(kv_ref.dtype)
    gb_ref[...] = r[:, kv_cols:]


def _in_proj_tail(h, w, kv_cols, tm):
    t, d = h.shape
    n = w.shape[1]
    return pl.pallas_call(
        functools.partial(_tail_kernel, kv_cols=kv_cols),
        grid=(t // tm,),
        in_specs=[
            pl.BlockSpec((tm, d), lambda i: (i, 0)),
            pl.BlockSpec((d, n), lambda i: (0, 0)),
        ],
        out_specs=[
            pl.BlockSpec((tm, kv_cols), lambda i: (i, 0)),
            pl.BlockSpec((tm, n - kv_cols), lambda i: (i, 0)),
        ],
        out_shape=[
            jax.ShapeDtypeStruct((t, kv_cols), BF16),
            jax.ShapeDtypeStruct((t, n - kv_cols), F32),
        ],
        compiler_params=_params(("arbitrary",)),
        name="in_proj_tail",
    )(h, w)


INV_BASE = 8


def _inverse_masks(n):
    ii = lax.broadcasted_iota(jnp.int32, (n, n), 0)
    jj = lax.broadcasted_iota(jnp.int32, (n, n), 1)
    shift = INV_BASE.bit_length() - 1
    base = (ii >> shift) == (jj >> shift)
    levels = []
    blk = INV_BASE
    while blk < n:
        same_parent = (ii >> (shift + 1)) == (jj >> (shift + 1))
        levels.append(jnp.logical_and(same_parent, (ii >> shift) != (jj >> shift)))
        shift += 1
        blk *= 2
    return base, levels


def _unit_lower_inverse(lms, eye, masks):
    base, levels = masks
    ps = [jnp.where(base, lm, 0.0) for lm in lms]
    xs = [eye - p for p in ps]
    step = 2
    while step < INV_BASE:
        ps = [_mm(p, p) for p in ps]
        xs = [x + _mm(x, p) for x, p in zip(xs, ps)]
        step *= 2
    for off_mask in levels:
        ts = [_mm(x, jnp.where(off_mask, lm, 0.0)) for x, lm in zip(xs, lms)]
        xs = [x - _mm(t, x) for x, t in zip(xs, ts)]
    return xs


def _chunk_cumsum(g):
    rows = lax.broadcasted_iota(jnp.int32, g.shape, 0)
    s = 1
    while s < g.shape[0]:
        g = g + jnp.where(rows >= s, pltpu.roll(g, s, 0), 0.0)
        s *= 2
    return g


def _gdn_kernel(q_ref, k_ref, v_ref, z_ref, gb_ref, cw_ref, hp_ref, nw_ref, o_ref,
                halo_ref, state_ref, *, ts, heads):
    c = GDN_CHUNK
    hd = GDN_HEAD_DIM
    w = heads * hd
    t = pl.program_id(1)

    @pl.when(t == 0)
    def _():
        halo_ref[...] = jnp.zeros(halo_ref.shape, halo_ref.dtype)
        state_ref[...] = jnp.zeros(state_ref.shape, F32)

    ii = lax.broadcasted_iota(jnp.int32, (c, c), 0)
    jj = lax.broadcasted_iota(jnp.int32, (c, c), 1)
    causal = ii >= jj
    strict = ii > jj
    eye = jnp.where(ii == jj, 1.0, 0.0).astype(F32)
    inv_masks = _inverse_masks(c)
    norm_w = nw_ref[...]

    n_sh = CONV_K - 1
    rr = lax.broadcasted_iota(jnp.int32, (n_sh * c, CONV_HALO + c), 0)
    cc = lax.broadcasted_iota(jnp.int32, (n_sh * c, CONV_HALO + c), 1)
    log_c = c.bit_length() - 1
    shift_mat = jnp.where(cc == (rr & (c - 1)) + (rr >> log_c) + (CONV_HALO - n_sh), 1.0, 0.0).astype(BF16)
    ones_sum = jnp.ones((hd, hd), BF16)
    ones_mean = jnp.full((hd, hd), 1.0 / hd, BF16)

    refs = (q_ref, k_ref, v_ref)
    for ci in range(ts // c):
        r0 = ci * c
        gb = gb_ref[r0:r0 + c, :]
        beta_all = jax.nn.sigmoid(gb)
        g_all = -jnp.exp(hp_ref[0:1, :]) * _softplus(gb + hp_ref[1:2, :])
        gc_all = _chunk_cumsum(g_all)
        gc_t = jnp.concatenate([gc_all, jnp.zeros((LANES - c, LANES), F32)], axis=0).T
        glast = gc_all[c - 1:c, :]
        egc_all = jnp.exp(gc_all)
        ekd_all = jnp.exp(glast - gc_all)
        cdec_all = jnp.exp(glast)

        hs = range(heads)

        def window(part, h):
            cols = slice(h * hd, (h + 1) * hd)
            if ci == 0:
                prev = halo_ref[:, part * w + h * hd:part * w + (h + 1) * hd]
            else:
                prev = refs[part][r0 - CONV_HALO:r0, cols]
            return jnp.concatenate([prev, refs[part][r0:r0 + c, cols]], axis=0)

        shifted = [jnp.dot(shift_mat, jnp.concatenate([window(p, h) for p in range(3)], axis=1),
                           preferred_element_type=F32) for h in hs]

        def conv_silu(part, h):
            col = part * w + h * hd
            acc = refs[part][r0:r0 + c, h * hd:(h + 1) * hd].astype(F32) * cw_ref[n_sh:n_sh + 1, col:col + hd]
            for j in range(n_sh):
                acc = acc + shifted[h][j * c:(j + 1) * c, part * hd:(part + 1) * hd] * cw_ref[j:j + 1, col:col + hd]
            return _silu(acc)

        q = [conv_silu(0, h) for h in hs]
        k = [conv_silu(1, h) for h in hs]
        v = [conv_silu(2, h) for h in hs]
        q = [x * lax.rsqrt(_mm(x * x, ones_sum) + L2_EPS) * (hd ** -0.5) for x in q]
        k = [x * lax.rsqrt(_mm(x * x, ones_sum) + L2_EPS) for x in k]

        bcol = [beta_all[:, h:h + 1] for h in hs]
        gcol = [gc_all[:, heads + h:heads + h + 1] for h in hs]
        egcol = [egc_all[:, heads + h:heads + h + 1] for h in hs]
        ekdcol = [ekd_all[:, heads + h:heads + h + 1] for h in hs]
        cdec = [cdec_all[:, heads + h:heads + h + 1] for h in hs]
        grow = [gc_t[heads + h:heads + h + 1, 0:c] for h in hs]

        decay = [jnp.where(causal, jnp.exp(jnp.where(causal, gcol[h] - grow[h], 0.0)), 0.0) for h in hs]
        kb = [k[h] * bcol[h] for h in hs]
        both = [_mm_nt(jnp.concatenate([kb[h], q[h]], axis=0), k[h]) for h in hs]
        lm = [jnp.where(strict, both[h][0:c] * decay[h], 0.0) for h in hs]
        qkm = [both[h][c:2 * c] * decay[h] for h in hs]
        tinv = _unit_lower_inverse(lm, eye, inv_masks)
        sol = [_mm(tinv[h], jnp.concatenate([v[h] * bcol[h], kb[h] * egcol[h]], axis=1)) for h in hs]

        s_old = [state_ref[h] for h in hs]
        v_new = [sol[h][:, 0:hd] - _mm(sol[h][:, hd:2 * hd], s_old[h]) for h in hs]
        o_inter = [_mm(q[h] * egcol[h], s_old[h]) for h in hs]
        o = [o_inter[h] + _mm(qkm[h], v_new[h]) for h in hs]
        for h in hs:
            state_ref[h] = s_old[h] * cdec[h] + _mm_tn(k[h] * ekdcol[h], v_new[h])

        for h in hs:
            on = o[h] * lax.rsqrt(_mm(o[h] * o[h], ones_mean) + NORM_EPS) * norm_w
            z = z_ref[r0:r0 + c, h * hd:(h + 1) * hd].astype(F32)
            o_ref[r0:r0 + c, h * hd:(h + 1) * hd] = (on * _silu(z)).astype(o_ref.dtype)

    for part in range(3):
        halo_ref[:, part * w:(part + 1) * w] = refs[part][ts - CONV_HALO:ts, :]


def _gdn(u_main, gb, conv_w, head_params, norm_w, batch, heads, ts):
    t = u_main.shape[0]
    w = heads * GDN_HEAD_DIM
    nt = t // batch // ts

    def col(blk):
        return pl.BlockSpec((ts, w), lambda b, i, blk=blk: (b * nt + i, blk))

    return pl.pallas_call(
        functools.partial(_gdn_kernel, ts=ts, heads=heads),
        grid=(batch, nt),
        in_specs=[
            col(0), col(1), col(2), col(3),
            pl.BlockSpec((ts, LANES), lambda b, i: (b * nt + i, 0)),
            pl.BlockSpec((CONV_K, 3 * w), lambda b, i: (0, 0)),
            pl.BlockSpec((8, LANES), lambda b, i: (0, 0)),
            pl.BlockSpec((1, GDN_HEAD_DIM), lambda b, i: (0, 0)),
        ],
        out_specs=pl.BlockSpec((ts, w), lambda b, i: (b * nt + i, 0)),
        out_shape=jax.ShapeDtypeStruct((t, w), BF16),
        scratch_shapes=[
            pltpu.VMEM((CONV_HALO, 3 * w), BF16),
            pltpu.VMEM((heads, GDN_HEAD_DIM, GDN_HEAD_DIM), F32),
        ],
        compiler_params=_params(("arbitrary", "arbitrary")),
        name="gated_deltanet",
    )(u_main, u_main, u_main, u_main, gb, conv_w, head_params, norm_w)


def _rope_table_kernel(pos_ref, invf_ref, sign_ref, cos_ref, sin_ref):
    ang = pos_ref[...].astype(F32) * invf_ref[...]
    cos_ref[...] = jnp.cos(ang)
    sin_ref[...] = jnp.sin(ang) * sign_ref[...]


def _rope_tables(positions, tr):
    t = positions.size
    half = SWA_HEAD_DIM // 2
    inv_freq = ROPE_THETA ** (-jnp.arange(half, dtype=F32) / half)
    invf = jnp.tile(inv_freq, LANES // half).reshape(1, LANES)
    sign = jnp.tile(jnp.concatenate([-jnp.ones((half,), F32), jnp.ones((half,), F32)]),
                    LANES // SWA_HEAD_DIM).reshape(1, LANES)
    return pl.pallas_call(
        _rope_table_kernel,
        grid=(t // tr,),
        in_specs=[
            pl.BlockSpec((tr, 1), lambda i: (i, 0)),
            pl.BlockSpec((1, LANES), lambda i: (0, 0)),
            pl.BlockSpec((1, LANES), lambda i: (0, 0)),
        ],
        out_specs=[pl.BlockSpec((tr, LANES), lambda i: (i, 0))] * 2,
        out_shape=[jax.ShapeDtypeStruct((t, LANES), F32)] * 2,
        compiler_params=_params(("arbitrary",)),
        name="rope_tables",
    )(positions.reshape(t, 1), invf, sign)


def _swa_kernel(sink_ref, q_ref, z_ref, kv_ref, cos_ref, sin_ref, o_ref, kprev_ref, vprev_ref,
                *, q_heads, kv_heads):
    blk = pl.program_id(1)
    tq = SWA_BLOCK
    hd = SWA_HEAD_DIM
    group = q_heads // kv_heads
    kvw = kv_heads * hd

    @pl.when(blk == 0)
    def _():
        kprev_ref[...] = jnp.zeros(kprev_ref.shape, kprev_ref.dtype)
        vprev_ref[...] = jnp.zeros(vprev_ref.shape, vprev_ref.dtype)

    cos = cos_ref[...]
    sin = sin_ref[...]
    lane = lax.broadcasted_iota(jnp.int32, (tq, LANES), 1)
    low_half = lane < hd
    first_quarter = (lane & (hd - 1)) < (hd // 2)

    def rope(xs, cos_t, sin_t):
        partner = jnp.where(first_quarter, pltpu.roll(xs, LANES - hd // 2, 1), pltpu.roll(xs, hd // 2, 1))
        return xs * cos_t + partner * sin_t

    q_scale = (hd ** -0.5) * LOG2_E
    cos_q = cos * q_scale
    sin_q = sin * q_scale

    def dup(xs, head_in_slab):
        swapped = pltpu.roll(xs, hd, 1)
        return jnp.where(low_half, xs, swapped) if head_in_slab == 0 else jnp.where(low_half, swapped, xs)

    qi = lax.broadcasted_iota(jnp.int32, (tq, 2 * tq), 0)
    kj = lax.broadcasted_iota(jnp.int32, (tq, 2 * tq), 1)
    lowest = jnp.maximum(qi, jnp.where(blk > 0, -1, tq - 1))
    valid = jnp.logical_and(kj > lowest, kj <= qi + tq)

    kv_per_slab = LANES // hd

    def scores_of_group(g):
        slab = g // kv_per_slab
        k_slab = rope(kv_ref[:, slab * LANES:(slab + 1) * LANES].astype(F32), cos, sin)
        v_slab = kv_ref[:, kvw + slab * LANES:kvw + (slab + 1) * LANES].astype(F32)
        k_cur = dup(k_slab, g % kv_per_slab).astype(BF16)
        v_cur = dup(v_slab, g % kv_per_slab).astype(BF16)
        k_all = jnp.concatenate([kprev_ref[g], k_cur], axis=0)
        v_all = jnp.concatenate([vprev_ref[g], v_cur], axis=0)
        q_parts = []
        for qs in range(g * group // 2, (g + 1) * group // 2):
            q_slab = rope(q_ref[:, qs * LANES:(qs + 1) * LANES].astype(F32), cos_q, sin_q)
            q_parts.append(jnp.where(low_half, q_slab, 0.0).astype(BF16))
            q_parts.append(jnp.where(low_half, 0.0, q_slab).astype(BF16))
        scores = _mm_nt(jnp.concatenate(q_parts, axis=0), k_all)
        return k_cur, v_cur, v_all, scores

    def finish_group(g, k_cur, v_cur, v_all, scores):
        p_parts, inv_denoms = [], []
        for hh in range(group):
            sink = sink_ref[g * group + hh] * LOG2_E
            s = jnp.where(valid, scores[hh * tq:(hh + 1) * tq], MASK_VALUE)
            m = jnp.maximum(jnp.max(s, axis=-1, keepdims=True), sink)
            p = jnp.exp2(s - m)
            denom = jnp.sum(p, axis=-1, keepdims=True) + jnp.exp2(sink - m)
            p_parts.append(p.astype(BF16))
            inv_denoms.append(1.0 / denom)
        pv = jnp.dot(jnp.concatenate(p_parts, axis=0), v_all, preferred_element_type=F32)
        for hh in range(0, group, 2):
            qs = (g * group + hh) // 2
            o_even = pv[hh * tq:(hh + 1) * tq] * inv_denoms[hh]
            o_odd = pv[(hh + 1) * tq:(hh + 2) * tq] * inv_denoms[hh + 1]
            z = z_ref[:, qs * LANES:(qs + 1) * LANES].astype(F32)
            o_ref[:, qs * LANES:(qs + 1) * LANES] = (jnp.where(low_half, o_even, o_odd) * _silu(z)).astype(o_ref.dtype)
        kprev_ref[g] = k_cur
        vprev_ref[g] = v_cur

    pending = scores_of_group(0)
    for g in range(kv_heads):
        current = pending
        if g + 1 < kv_heads:
            pending = scores_of_group(g + 1)
        finish_group(g, *current)


def _swa(u_main, kv, cos_t, sin_t, sinks, batch, q_heads, kv_heads, q_blk, z_blk):
    t = u_main.shape[0]
    w = q_heads * SWA_HEAD_DIM
    nb = t // batch // SWA_BLOCK

    def row(b, i):
        return b * nb + i

    return pl.pallas_call(
        functools.partial(_swa_kernel, q_heads=q_heads, kv_heads=kv_heads),
        grid=(batch, nb),
        in_specs=[
            pl.BlockSpec(memory_space=pltpu.SMEM),
            pl.BlockSpec((SWA_BLOCK, w), lambda b, i: (row(b, i), q_blk)),
            pl.BlockSpec((SWA_BLOCK, w), lambda b, i: (row(b, i), z_blk)),
            pl.BlockSpec((SWA_BLOCK, kv.shape[1]), lambda b, i: (row(b, i), 0)),
            pl.BlockSpec((SWA_BLOCK, LANES), lambda b, i: (row(b, i), 0)),
            pl.BlockSpec((SWA_BLOCK, LANES), lambda b, i: (row(b, i), 0)),
        ],
        out_specs=pl.BlockSpec((SWA_BLOCK, w), lambda b, i: (row(b, i), 0)),
        out_shape=jax.ShapeDtypeStruct((t, w), BF16),
        scratch_shapes=[
            pltpu.VMEM((kv_heads, SWA_BLOCK, LANES), BF16),
            pltpu.VMEM((kv_heads, SWA_BLOCK, LANES), BF16),
        ],
        compiler_params=_params(("arbitrary", "arbitrary")),
        name="sliding_window_attention",
    )(sinks, u_main, u_main, kv, cos_t, sin_t)


def _merge_kernel(oa_ref, ob_ref, ga_ref, gb_ref, pa_ref, pb_ref, y_ref):
    ya = jnp.dot(oa_ref[...], pa_ref[...], preferred_element_type=F32)
    yb = jnp.dot(ob_ref[...], pb_ref[...], preferred_element_type=F32)
    y = jax.nn.sigmoid(ga_ref[...].astype(F32)) * ya + jax.nn.sigmoid(gb_ref[...].astype(F32)) * yb
    y_ref[...] = y.astype(y_ref.dtype)


def _merge(o_a, o_b, u_main, proj_a, proj_b, ga_blk, gb_blk, tm):
    t, d = o_a.shape
    resident = dict(pipeline_mode=pl.Buffered(1))
    return pl.pallas_call(
        _merge_kernel,
        grid=(t // tm,),
        in_specs=[
            pl.BlockSpec((tm, d), lambda i: (i, 0)),
            pl.BlockSpec((tm, d), lambda i: (i, 0)),
            pl.BlockSpec((tm, d), lambda i: (i, ga_blk)),
            pl.BlockSpec((tm, d), lambda i: (i, gb_blk)),
            pl.BlockSpec(proj_a.shape, lambda i: (0, 0), **resident),
            pl.BlockSpec(proj_b.shape, lambda i: (0, 0), **resident),
        ],
        out_specs=pl.BlockSpec((tm, d), lambda i: (i, 0)),
        out_shape=jax.ShapeDtypeStruct((t, d), BF16),
        compiler_params=_params(("arbitrary",)),
        name="gated_merge",
    )(o_a, o_b, u_main, u_main, proj_a, proj_b)


def _out_kernel(y_ref, x_ref, w_ref, mod_ref, nmod_ref, nw_ref, *out_refs, final):
    r = jnp.dot(y_ref[...], w_ref[...], preferred_element_type=F32)
    x_new = x_ref[...] + mod_ref[0, 2:3, :] * r
    if final:
        (o_ref,) = out_refs
        o_ref[...] = x_new * lax.rsqrt(jnp.mean(x_new * x_new, axis=-1, keepdims=True) + NORM_EPS) * nw_ref[...]
    else:
        x_out_ref, h_ref = out_refs
        x_out_ref[...] = x_new
        h_ref[...] = _modulated_norm(x_new, nw_ref[...], nmod_ref[0]).astype(h_ref.dtype)


def _out_proj(y, x2, w_out, mod, next_mod, next_nw, batch, tm, final):
    t, d = x2.shape
    nt = t // batch // tm

    def row(b, i):
        return (b * nt + i, 0)

    if final:
        out_specs = pl.BlockSpec((tm, d), row)
        out_shape = jax.ShapeDtypeStruct((t, d), F32)
    else:
        out_specs = [pl.BlockSpec((tm, d), row)] * 2
        out_shape = [jax.ShapeDtypeStruct((t, d), F32), jax.ShapeDtypeStruct((t, d), BF16)]
    return pl.pallas_call(
        functools.partial(_out_kernel, final=final),
        grid=(batch, nt),
        in_specs=[
            pl.BlockSpec((tm, d), row),
            pl.BlockSpec((tm, d), row),
            pl.BlockSpec(w_out.shape, lambda b, i: (0, 0), pipeline_mode=pl.Buffered(1)),
            pl.BlockSpec((1, 3, d), lambda b, i: (b, 0, 0)),
            pl.BlockSpec((1, 3, d), lambda b, i: (b, 0, 0)),
            pl.BlockSpec((1, d), lambda b, i: (0, 0)),
        ],
        out_specs=out_specs,
        out_shape=out_shape,
        compiler_params=_params(("arbitrary", "arbitrary")),
        name="out_proj_final" if final else "out_proj",
    )(y, x2, w_out, mod, next_mod, next_nw)


def _tile(n, target, align):
    if n <= target:
        return n
    best = align
    for cand in range(align, target + 1, align):
        if n % cand == 0:
            best = cand
    assert n % best == 0
    return best


def kernel(x, c, positions, ada_w, ada_b, norm_w, w_in, conv_w, gdn_a_log, gdn_dt_bias, gdn_norm_w,
           swa_sinks, proj_a, proj_b, w_out, final_norm_w):
    batch, seq, d = x.shape
    depth = ada_w.shape[0]
    t = batch * seq
    gw = conv_w.shape[-1] // 3
    gh = gdn_a_log.shape[-1]
    sw = proj_b.shape[1]
    q_heads = swa_sinks.shape[-1]
    in_cols = w_in.shape[-1]
    kvw = (in_cols - 4 * gw - 2 * gh - 2 * sw - 2 * d) // 2
    kv_heads = kvw // SWA_HEAD_DIM
    assert gw == d and sw == d, "column-block layout assumes both branch widths equal d_model"
    assert gw == gh * GDN_HEAD_DIM and sw == q_heads * SWA_HEAD_DIM
    assert 2 * gh <= LANES and kvw % LANES == 0 and q_heads % (2 * kv_heads) == 0
    assert seq % SWA_BLOCK == 0 and seq % (2 * GDN_CHUNK) == 0

    sizes = (3 * gw, gh, gh, gw, sw, kvw, kvw, sw, d, d)
    offs = [0]
    for s in sizes:
        offs.append(offs[-1] + s)
    blk_qb, blk_zb, blk_ga, blk_gb = 4, 5, 6, 7

    def projection_weights(w):
        qkv_a, beta_a, a_a, z_a, q_b, k_b, v_b, z_b, g_a, g_b = [w[:, offs[i]:offs[i + 1]] for i in range(len(sizes))]
        w_main = jnp.concatenate([qkv_a, z_a, q_b, z_b, g_a, g_b], axis=-1).astype(BF16)
        pad = jnp.zeros((d, LANES - 2 * gh), w.dtype)
        w_tail = jnp.concatenate([k_b, v_b, beta_a, a_a, pad], axis=-1).astype(BF16)
        return w_main, w_tail

    head_params = jnp.zeros((depth, 8, LANES), F32)
    head_params = head_params.at[:, 0, gh:2 * gh].set(gdn_a_log.astype(F32))
    head_params = head_params.at[:, 1, gh:2 * gh].set(gdn_dt_bias.astype(F32))

    tm_big = _tile(seq, 1024, 8)
    tm_mid = _tile(seq, 512, 8)
    mod = _modulation(c, ada_w, ada_b, _tile(3 * d, 1024, LANES)).reshape(depth, batch, 3, d)
    cos_t, sin_t = _rope_tables(positions, _tile(t, 512, 8))

    x2 = x.reshape(t, d)
    h = _prep(x2, mod[0], norm_w[0:1], batch, _tile(seq, 512, 8))
    out = None
    for l in range(depth):
        w_main, w_tail = projection_weights(w_in[l])
        u_main = _in_proj_main(h, w_main, tm_big, _tile(8 * d, 2048, LANES))
        kv, gb = _in_proj_tail(h, w_tail, 2 * kvw, tm_big)
        o_a = _gdn(u_main, gb, conv_w[l], head_params[l], gdn_norm_w[l:l + 1], batch, gh, 2 * GDN_CHUNK)
        o_b = _swa(u_main, kv, cos_t, sin_t, swa_sinks[l], batch, q_heads, kv_heads, blk_qb, blk_zb)
        y = _merge(o_a, o_b, u_main, proj_a[l].astype(BF16), proj_b[l].astype(BF16), blk_ga, blk_gb, tm_mid)
        final = l == depth - 1
        if final:
            out = _out_proj(y, x2, w_out[l].astype(BF16), mod[l], mod[l], final_norm_w.reshape(1, d), batch, tm_mid, True)
        else:
            x2, h = _out_proj(y, x2, w_out[l].astype(BF16), mod[l], mod[l + 1], norm_w[l + 1:l + 2], batch, tm_mid, False)
    return out.reshape(batch, seq, d)
```

```python
import functools

import jax
import jax.numpy as jnp
from jax import lax
from jax.experimental import pallas as pl
from jax.experimental.pallas import tpu as pltpu

F32 = jnp.float32
BF16 = jnp.bfloat16

LANES = 128
GDN_HEAD_DIM = 128
GDN_CHUNK = 64
CONV_K = 4
CONV_HALO = 16
SWA_HEAD_DIM = 64
SWA_BLOCK = 128
ROPE_THETA = 10000.0
NORM_EPS = 1e-6
L2_EPS = 1e-6
MASK_VALUE = -1e30
LOG2_E = 1.4426950408889634
VMEM_LIMIT = 56 * 1024 * 1024


def _params(semantics):
    return pltpu.CompilerParams(dimension_semantics=semantics, vmem_limit_bytes=VMEM_LIMIT)


def _mm(a, b):
    return jnp.dot(a.astype(BF16), b.astype(BF16), preferred_element_type=F32)


def _mm_nt(a, b):
    return lax.dot_general(a.astype(BF16), b.astype(BF16), (((1,), (1,)), ((), ())),
                           preferred_element_type=F32)


def _mm_tn(a, b):
    return lax.dot_general(a.astype(BF16), b.astype(BF16), (((0,), (0,)), ((), ())),
                           preferred_element_type=F32)


def _silu(x):
    return x * jax.nn.sigmoid(x)


def _softplus(x):
    return jnp.maximum(x, 0.0) + jnp.log1p(jnp.exp(-jnp.abs(x)))


def _mod_kernel(c_ref, w_ref, b_ref, o_ref):
    c = c_ref[...]
    act = _silu(c)
    hi = act.astype(BF16)
    lo = (act - hi.astype(F32)).astype(BF16)
    w = w_ref[0].astype(BF16)
    acc = jnp.dot(hi, w, preferred_element_type=F32) + jnp.dot(lo, w, preferred_element_type=F32)
    o_ref[0] = acc + b_ref[0]


def _modulation(c, ada_w, ada_b, tn):
    depth, d, n = ada_w.shape
    b = c.shape[0]
    return pl.pallas_call(
        _mod_kernel,
        grid=(depth, n // tn),
        in_specs=[
            pl.BlockSpec((b, d), lambda l, j: (0, 0)),
            pl.BlockSpec((1, d, tn), lambda l, j: (l, 0, j)),
            pl.BlockSpec((1, 1, tn), lambda l, j: (l, 0, j)),
        ],
        out_specs=pl.BlockSpec((1, b, tn), lambda l, j: (l, 0, j)),
        out_shape=jax.ShapeDtypeStruct((depth, b, n), F32),
        compiler_params=_params(("arbitrary", "arbitrary")),
        name="adaln_modulation",
    )(c, ada_w, ada_b.reshape(depth, 1, n))


def _modulated_norm(x, nw, mod):
    y = x * lax.rsqrt(jnp.mean(x * x, axis=-1, keepdims=True) + NORM_EPS) * nw
    return y * (1.0 + mod[1:2, :]) + mod[0:1, :]


def _prep_kernel(x_ref, mod_ref, nw_ref, h_ref):
    h_ref[...] = _modulated_norm(x_ref[...], nw_ref[...], mod_ref[0]).astype(h_ref.dtype)


def _prep(x2, mod, nw, batch, tm):
    t, d = x2.shape
    nt = t // batch // tm
    return pl.pallas_call(
        _prep_kernel,
        grid=(batch, nt),
        in_specs=[
            pl.BlockSpec((tm, d), lambda b, i: (b * nt + i, 0)),
            pl.BlockSpec((1, 3, d), lambda b, i: (b, 0, 0)),
            pl.BlockSpec((1, d), lambda b, i: (0, 0)),
        ],
        out_specs=pl.BlockSpec((tm, d), lambda b, i: (b * nt + i, 0)),
        out_shape=jax.ShapeDtypeStruct((t, d), BF16),
        compiler_params=_params(("arbitrary", "arbitrary")),
        name="modulated_norm",
    )(x2, mod, nw)


def _matmul_kernel(a_ref, w_ref, o_ref):
    o_ref[...] = jnp.dot(a_ref[...], w_ref[...], preferred_element_type=F32).astype(o_ref.dtype)


def _in_proj_main(h, w, layer, tm, tn):
    t, d = h.shape
    n = w.shape[-1]
    return pl.pallas_call(
        _matmul_kernel,
        grid=(n // tn, t // tm),
        in_specs=[
            pl.BlockSpec((tm, d), lambda j, i: (i, 0)),
            pl.BlockSpec((None, d, tn), lambda j, i: (layer, 0, j)),
        ],
        out_specs=pl.BlockSpec((tm, tn), lambda j, i: (i, j)),
        out_shape=jax.ShapeDtypeStruct((t, n), BF16),
        compiler_params=_params(("arbitrary", "arbitrary")),
        name="in_proj_main",
    )(h, w)


def _tail_kernel(a_ref, w_ref, kv_ref, gb_ref, *, kv_cols):
    r = jnp.dot(a_ref[...], w_ref[...], preferred_element_type=F32)
    kv_ref[...] = r[:, :kv_cols].astype(kv_ref.dtype)
    gb_ref[...] = r[:, kv_cols:]


def _in_proj_tail(h, w, layer, kv_cols, tm):
    t, d = h.shape
    n = w.shape[-1]
    return pl.pallas_call(
        functools.partial(_tail_kernel, kv_cols=kv_cols),
        grid=(t // tm,),
        in_specs=[
            pl.BlockSpec((tm, d), lambda i: (i, 0)),
            pl.BlockSpec((None, d, n), lambda i: (layer, 0, 0)),
        ],
        out_specs=[
            pl.BlockSpec((tm, kv_cols), lambda i: (i, 0)),
            pl.BlockSpec((tm, n - kv_cols), lambda i: (i, 0)),
        ],
        out_shape=[
            jax.ShapeDtypeStruct((t, kv_cols), BF16),
            jax.ShapeDtypeStruct((t, n - kv_cols), F32),
        ],
        compiler_params=_params(("arbitrary",)),
        name="in_proj_tail",
    )(h, w)


INV_BASE = 8


def _pair_iotas(n):
    ii = lax.broadcasted_iota(jnp.int32, (n, 2 * n), 0)
    jj = lax.broadcasted_iota(jnp.int32, (n, 2 * n), 1) & (n - 1)
    return ii, jj


def _inverse_masks(n):
    ii, jj = _pair_iotas(n)
    shift = INV_BASE.bit_length() - 1
    base = (ii >> shift) == (jj >> shift)
    levels = []
    blk = INV_BASE
    while blk < n:
        same_parent = (ii >> (shift + 1)) == (jj >> (shift + 1))
        levels.append(jnp.logical_and(same_parent, (ii >> shift) != (jj >> shift)))
        shift += 1
        blk *= 2
    return base, levels


def _mm_pair(xp, pp, low_half):
    blockdiag = jnp.concatenate([jnp.where(low_half, pp, 0.0), jnp.where(low_half, 0.0, pp)], axis=0)
    return _mm(xp, blockdiag)


def _unit_lower_inverse(lms, eye, masks, low_half):
    base, levels = masks
    ps = [jnp.where(base, lm, 0.0) for lm in lms]
    xs = [eye - p for p in ps]
    step = 2
    while step < INV_BASE:
        ps = [_mm_pair(p, p, low_half) for p in ps]
        xs = [x + _mm_pair(x, p, low_half) for x, p in zip(xs, ps)]
        step *= 2
    for off_mask in levels:
        ts = [_mm_pair(x, jnp.where(off_mask, lm, 0.0), low_half) for x, lm in zip(xs, lms)]
        xs = [x - _mm_pair(t, x, low_half) for x, t in zip(xs, ts)]
    return xs


def _chunk_cumsum(g):
    rows = lax.broadcasted_iota(jnp.int32, g.shape, 0)
    s = 1
    while s < g.shape[0]:
        g = g + jnp.where(rows >= s, pltpu.roll(g, s, 0), 0.0)
        s *= 2
    return g


def _gdn_kernel(q_ref, k_ref, v_ref, z_ref, gb_ref, cw_ref, hp_ref, nw_ref, o_ref,
                halo_ref, state_ref, *, ts, heads):
    c = GDN_CHUNK
    hd = GDN_HEAD_DIM
    w = heads * hd
    t = pl.program_id(1)

    @pl.when(t == 0)
    def _():
        halo_ref[...] = jnp.zeros(halo_ref.shape, halo_ref.dtype)
        state_ref[...] = jnp.zeros(state_ref.shape, F32)

    ii, jj = _pair_iotas(c)
    causal = ii >= jj
    strict = ii > jj
    eye = jnp.where(ii == jj, 1.0, 0.0).astype(F32)
    inv_masks = _inverse_masks(c)
    low_half = lax.broadcasted_iota(jnp.int32, (c, 2 * c), 1) < c
    zeros_hd = jnp.zeros((c, hd), F32)
    norm_w = nw_ref[...]

    n_sh = CONV_K - 1
    rr = lax.broadcasted_iota(jnp.int32, (n_sh * c, CONV_HALO + c), 0)
    cc = lax.broadcasted_iota(jnp.int32, (n_sh * c, CONV_HALO + c), 1)
    log_c = c.bit_length() - 1
    shift_mat = jnp.where(cc == (rr & (c - 1)) + (rr >> log_c) + (CONV_HALO - n_sh), 1.0, 0.0).astype(BF16)
    ones_sum = jnp.ones((hd, hd), BF16)
    ones_mean = jnp.full((hd, hd), 1.0 / hd, BF16)

    refs = (q_ref, k_ref, v_ref)
    n_chunks = ts // c
    items = [(ci, h) for ci in range(n_chunks) for h in range(heads)]
    pair_items = [(ci, h) for ci in range(n_chunks) for h in range(0, heads, 2)]
    at = {it: n for n, it in enumerate(items)}

    beta_all, gc_all, gc_t, egc_all, ekd_all, cdec_all = [], [], [], [], [], []
    for ci in range(n_chunks):
        gb = gb_ref[ci * c:(ci + 1) * c, :]
        beta_all.append(jax.nn.sigmoid(gb))
        gc = _chunk_cumsum(-jnp.exp(hp_ref[0:1, :]) * _softplus(gb + hp_ref[1:2, :]))
        gc_all.append(gc)
        gc_t.append(jnp.concatenate([gc, pltpu.roll(gc, LANES - 1, 1)], axis=0).T)
        glast = gc[c - 1:c, :]
        egc_all.append(jnp.exp(gc))
        ekd_all.append(jnp.exp(glast - gc))
        cdec_all.append(jnp.exp(glast))

    def window(part, ci, h):
        cols = slice(h * hd, (h + 1) * hd)
        if ci == 0:
            prev = halo_ref[:, part * w + h * hd:part * w + (h + 1) * hd]
        else:
            prev = refs[part][ci * c - CONV_HALO:ci * c, cols]
        return jnp.concatenate([prev, refs[part][ci * c:(ci + 1) * c, cols]], axis=0)

    shifted = [jnp.dot(shift_mat, jnp.concatenate([window(p, ci, h) for p in range(3)], axis=1),
                       preferred_element_type=F32) for ci, h in items]

    def conv_silu(part, ci, h):
        col = part * w + h * hd
        acc = refs[part][ci * c:(ci + 1) * c, h * hd:(h + 1) * hd].astype(F32) * cw_ref[n_sh:n_sh + 1, col:col + hd]
        for j in range(n_sh):
            acc = acc + shifted[at[ci, h]][j * c:(j + 1) * c, part * hd:(part + 1) * hd] * cw_ref[j:j + 1, col:col + hd]
        return _silu(acc)

    q = [conv_silu(0, ci, h) for ci, h in items]
    k = [conv_silu(1, ci, h) for ci, h in items]
    v = [conv_silu(2, ci, h) for ci, h in items]
    ssq = _mm(jnp.concatenate([x * x for x in q + k], axis=0), ones_sum)
    n_it = len(items)
    q = [x * lax.rsqrt(ssq[n * c:(n + 1) * c] + L2_EPS) * (hd ** -0.5) for n, x in enumerate(q)]
    k = [x * lax.rsqrt(ssq[(n_it + n) * c:(n_it + n + 1) * c] + L2_EPS) for n, x in enumerate(k)]

    bcol = [beta_all[ci][:, h:h + 1] for ci, h in items]
    gcol = [gc_all[ci][:, heads + h:heads + h + 1] for ci, h in items]
    egcol = [egc_all[ci][:, heads + h:heads + h + 1] for ci, h in items]
    ekdcol = [ekd_all[ci][:, heads + h:heads + h + 1] for ci, h in items]
    cdec = [cdec_all[ci][:, heads + h:heads + h + 1] for ci, h in items]
    kb = [kk * b for kk, b in zip(k, bcol)]
    rhs = [jnp.concatenate([vv * b, kk * e], axis=1) for vv, b, kk, e in zip(v, bcol, kb, egcol)]

    decay = [jnp.where(causal, jnp.exp(jnp.where(causal, jnp.where(low_half, gcol[at[ci, h]], gcol[at[ci, h + 1]])
                                                 - gc_t[ci][heads + h:heads + h + 1, :], 0.0)), 0.0)
             for ci, h in pair_items]
    both = [_mm_nt(jnp.concatenate([jnp.concatenate([kb[at[ci, h]], kb[at[ci, h + 1]]], axis=1),
                                    jnp.concatenate([q[at[ci, h]], q[at[ci, h + 1]]], axis=1)], axis=0),
                   jnp.concatenate([jnp.concatenate([k[at[ci, h]], zeros_hd], axis=1),
                                    jnp.concatenate([zeros_hd, k[at[ci, h + 1]]], axis=1)], axis=0))
            for ci, h in pair_items]
    lm = [jnp.where(strict, b[0:c] * d, 0.0) for b, d in zip(both, decay)]
    qkm_p = [b[c:2 * c] * d for b, d in zip(both, decay)]
    tinv = _unit_lower_inverse(lm, eye, inv_masks, low_half)
    sol_p = [_mm(jnp.concatenate([jnp.where(low_half, tp, 0.0), jnp.where(low_half, 0.0, tp)], axis=0),
                 jnp.concatenate([rhs[at[ci, h]], rhs[at[ci, h + 1]]], axis=0))
             for tp, (ci, h) in zip(tinv, pair_items)]
    q_dec = [qq * e for qq, e in zip(q, egcol)]
    k_dec = [kk * e for kk, e in zip(k, ekdcol)]

    hs = range(heads)
    pairs = range(0, heads, 2)
    for ci in range(n_chunks):
        sol = [sol_p[(ci * heads + h) // 2][(h % 2) * c:(h % 2 + 1) * c] for h in hs]
        s_old = [state_ref[h] for h in hs]
        ws_qs = [_mm(jnp.concatenate([sol[h][:, hd:2 * hd], q_dec[at[ci, h]]], axis=0), s_old[h]) for h in hs]
        v_new = [sol[h][:, 0:hd] - ws_qs[h][0:c] for h in hs]
        o_inter = [ws_qs[h][c:2 * c] for h in hs]
        o_intra_p = [_mm(jnp.concatenate([jnp.where(low_half, qkm_p[(ci * heads + h) // 2], 0.0),
                                          jnp.where(low_half, 0.0, qkm_p[(ci * heads + h) // 2])], axis=0),
                         jnp.concatenate([v_new[h], v_new[h + 1]], axis=0)) for h in pairs]
        o = [o_inter[h] + o_intra_p[h // 2][(h % 2) * c:(h % 2 + 1) * c] for h in hs]
        for h in hs:
            state_ref[h] = s_old[h] * cdec[at[ci, h]] + _mm_tn(k_dec[at[ci, h]], v_new[h])
        msq = _mm(jnp.concatenate([x * x for x in o], axis=0), ones_mean)
        for h in hs:
            on = o[h] * lax.rsqrt(msq[h * c:(h + 1) * c] + NORM_EPS) * norm_w
            z = z_ref[ci * c:(ci + 1) * c, h * hd:(h + 1) * hd].astype(F32)
            o_ref[ci * c:(ci + 1) * c, h * hd:(h + 1) * hd] = (on * _silu(z)).astype(o_ref.dtype)

    for part in range(3):
        halo_ref[:, part * w:(part + 1) * w] = refs[part][ts - CONV_HALO:ts, :]


def _gdn(u_main, gb, conv_w, head_params, norm_w, batch, heads, ts):
    t = u_main.shape[0]
    w = heads * GDN_HEAD_DIM
    nt = t // batch // ts

    def col(blk):
        return pl.BlockSpec((ts, w), lambda b, i, blk=blk: (b * nt + i, blk))

    return pl.pallas_call(
        functools.partial(_gdn_kernel, ts=ts, heads=heads),
        grid=(batch, nt),
        in_specs=[
            col(0), col(1), col(2), col(3),
            pl.BlockSpec((ts, LANES), lambda b, i: (b * nt + i, 0)),
            pl.BlockSpec((CONV_K, 3 * w), lambda b, i: (0, 0)),
            pl.BlockSpec((8, LANES), lambda b, i: (0, 0)),
            pl.BlockSpec((1, GDN_HEAD_DIM), lambda b, i: (0, 0)),
        ],
        out_specs=pl.BlockSpec((ts, w), lambda b, i: (b * nt + i, 0)),
        out_shape=jax.ShapeDtypeStruct((t, w), BF16),
        scratch_shapes=[
            pltpu.VMEM((CONV_HALO, 3 * w), BF16),
            pltpu.VMEM((heads, GDN_HEAD_DIM, GDN_HEAD_DIM), F32),
        ],
        compiler_params=_params(("arbitrary", "arbitrary")),
        name="gated_deltanet",
    )(u_main, u_main, u_main, u_main, gb, conv_w, head_params, norm_w)


def _rope_table_kernel(pos_ref, invf_ref, sign_ref, cos_ref, sin_ref):
    ang = pos_ref[...].astype(F32) * invf_ref[...]
    cos_ref[...] = jnp.cos(ang)
    sin_ref[...] = jnp.sin(ang) * sign_ref[...]


def _rope_tables(positions, tr):
    t = positions.size
    half = SWA_HEAD_DIM // 2
    inv_freq = ROPE_THETA ** (-jnp.arange(half, dtype=F32) / half)
    invf = jnp.tile(inv_freq, LANES // half).reshape(1, LANES)
    sign = jnp.tile(jnp.concatenate([-jnp.ones((half,), F32), jnp.ones((half,), F32)]),
                    LANES // SWA_HEAD_DIM).reshape(1, LANES)
    return pl.pallas_call(
        _rope_table_kernel,
        grid=(t // tr,),
        in_specs=[
            pl.BlockSpec((tr, 1), lambda i: (i, 0)),
            pl.BlockSpec((1, LANES), lambda i: (0, 0)),
            pl.BlockSpec((1, LANES), lambda i: (0, 0)),
        ],
        out_specs=[pl.BlockSpec((tr, LANES), lambda i: (i, 0))] * 2,
        out_shape=[jax.ShapeDtypeStruct((t, LANES), F32)] * 2,
        compiler_params=_params(("arbitrary",)),
        name="rope_tables",
    )(positions.reshape(t, 1), invf, sign)


def _swa_kernel(sink_ref, q_ref, z_ref, kv_ref, cos_ref, sin_ref, o_ref, kprev_ref, vprev_ref,
                *, q_heads, kv_heads):
    blk = pl.program_id(1)
    tq = SWA_BLOCK
    hd = SWA_HEAD_DIM
    group = q_heads // kv_heads
    kvw = kv_heads * hd

    @pl.when(blk == 0)
    def _():
        kprev_ref[...] = jnp.zeros(kprev_ref.shape, kprev_ref.dtype)
        vprev_ref[...] = jnp.zeros(vprev_ref.shape, vprev_ref.dtype)

    cos = cos_ref[...]
    sin = sin_ref[...]
    lane = lax.broadcasted_iota(jnp.int32, (tq, LANES), 1)
    low_half = lane < hd
    first_quarter = (lane & (hd - 1)) < (hd // 2)

    def rope(xs, cos_t, sin_t):
        partner = jnp.where(first_quarter, pltpu.roll(xs, LANES - hd // 2, 1), pltpu.roll(xs, hd // 2, 1))
        return xs * cos_t + partner * sin_t

    q_scale = (hd ** -0.5) * LOG2_E
    cos_q = cos * q_scale
    sin_q = sin * q_scale

    def dup(xs, head_in_slab):
        swapped = pltpu.roll(xs, hd, 1)
        return jnp.where(low_half, xs, swapped) if head_in_slab == 0 else jnp.where(low_half, swapped, xs)

    qi = lax.broadcasted_iota(jnp.int32, (tq, 2 * tq), 0)
    kj = lax.broadcasted_iota(jnp.int32, (tq, 2 * tq), 1)
    lowest = jnp.maximum(qi, jnp.where(blk > 0, -1, tq - 1))
    valid = jnp.logical_and(kj > lowest, kj <= qi + tq)

    kv_per_slab = LANES // hd

    def scores_of_group(g):
        slab = g // kv_per_slab
        k_slab = rope(kv_ref[:, slab * LANES:(slab + 1) * LANES].astype(F32), cos, sin)
        v_slab = kv_ref[:, kvw + slab * LANES:kvw + (slab + 1) * LANES].astype(F32)
        k_cur = dup(k_slab, g % kv_per_slab).astype(BF16)
        v_cur = dup(v_slab, g % kv_per_slab).astype(BF16)
        k_all = jnp.concatenate([kprev_ref[g], k_cur], axis=0)
        v_all = jnp.concatenate([vprev_ref[g], v_cur], axis=0)
        q_parts = []
        for qs in range(g * group // 2, (g + 1) * group // 2):
            q_slab = rope(q_ref[:, qs * LANES:(qs + 1) * LANES].astype(F32), cos_q, sin_q)
            q_parts.append(jnp.where(low_half, q_slab, 0.0).astype(BF16))
            q_parts.append(jnp.where(low_half, 0.0, q_slab).astype(BF16))
        scores = _mm_nt(jnp.concatenate(q_parts, axis=0), k_all)
        return k_cur, v_cur, v_all, scores

    def finish_group(g, k_cur, v_cur, v_all, scores):
        p_parts, inv_denoms = [], []
        for hh in range(group):
            sink = sink_ref[g * group + hh] * LOG2_E
            s = jnp.where(valid, scores[hh * tq:(hh + 1) * tq], MASK_VALUE)
            m = jnp.maximum(jnp.max(s, axis=-1, keepdims=True), sink)
            p = jnp.exp2(s - m)
            denom = jnp.sum(p, axis=-1, keepdims=True) + jnp.exp2(sink - m)
            p_parts.append(p.astype(BF16))
            inv_denoms.append(1.0 / denom)
        pv = jnp.dot(jnp.concatenate(p_parts, axis=0), v_all, preferred_element_type=F32)
        for hh in range(0, group, 2):
            qs = (g * group + hh) // 2
            o_even = pv[hh * tq:(hh + 1) * tq] * inv_denoms[hh]
            o_odd = pv[(hh + 1) * tq:(hh + 2) * tq] * inv_denoms[hh + 1]
            z = z_ref[:, qs * LANES:(qs + 1) * LANES].astype(F32)
            o_ref[:, qs * LANES:(qs + 1) * LANES] = (jnp.where(low_half, o_even, o_odd) * _silu(z)).astype(o_ref.dtype)
        kprev_ref[g] = k_cur
        vprev_ref[g] = v_cur

    pending = scores_of_group(0)
    for g in range(kv_heads):
        current = pending
        if g + 1 < kv_heads:
            pending = scores_of_group(g + 1)
        finish_group(g, *current)


def _swa(u_main, kv, cos_t, sin_t, sinks, batch, q_heads, kv_heads, q_blk, z_blk):
    t = u_main.shape[0]
    w = q_heads * SWA_HEAD_DIM
    nb = t // batch // SWA_BLOCK

    def row(b, i):
        return b * nb + i

    return pl.pallas_call(
        functools.partial(_swa_kernel, q_heads=q_heads, kv_heads=kv_heads),
        grid=(batch, nb),
        in_specs=[
            pl.BlockSpec(memory_space=pltpu.SMEM),
            pl.BlockSpec((SWA_BLOCK, w), lambda b, i: (row(b, i), q_blk)),
            pl.BlockSpec((SWA_BLOCK, w), lambda b, i: (row(b, i), z_blk)),
            pl.BlockSpec((SWA_BLOCK, kv.shape[1]), lambda b, i: (row(b, i), 0)),
            pl.BlockSpec((SWA_BLOCK, LANES), lambda b, i: (row(b, i), 0)),
            pl.BlockSpec((SWA_BLOCK, LANES), lambda b, i: (row(b, i), 0)),
        ],
        out_specs=pl.BlockSpec((SWA_BLOCK, w), lambda b, i: (row(b, i), 0)),
        out_shape=jax.ShapeDtypeStruct((t, w), BF16),
        scratch_shapes=[
            pltpu.VMEM((kv_heads, SWA_BLOCK, LANES), BF16),
            pltpu.VMEM((kv_heads, SWA_BLOCK, LANES), BF16),
        ],
        compiler_params=_params(("arbitrary", "arbitrary")),
        name="sliding_window_attention",
    )(sinks, u_main, u_main, kv, cos_t, sin_t)


def _merge_kernel(oa_ref, ob_ref, ga_ref, gb_ref, pa_ref, pb_ref, y_ref):
    ya = jnp.dot(oa_ref[...], pa_ref[...], preferred_element_type=F32)
    yb = jnp.dot(ob_ref[...], pb_ref[...], preferred_element_type=F32)
    y = jax.nn.sigmoid(ga_ref[...].astype(F32)) * ya + jax.nn.sigmoid(gb_ref[...].astype(F32)) * yb
    y_ref[...] = y.astype(y_ref.dtype)


def _merge(o_a, o_b, u_main, proj_a, proj_b, ga_blk, gb_blk, tm):
    t, d = o_a.shape
    resident = dict(pipeline_mode=pl.Buffered(1))
    return pl.pallas_call(
        _merge_kernel,
        grid=(t // tm,),
        in_specs=[
            pl.BlockSpec((tm, d), lambda i: (i, 0)),
            pl.BlockSpec((tm, d), lambda i: (i, 0)),
            pl.BlockSpec((tm, d), lambda i: (i, ga_blk)),
            pl.BlockSpec((tm, d), lambda i: (i, gb_blk)),
            pl.BlockSpec(proj_a.shape, lambda i: (0, 0), **resident),
            pl.BlockSpec(proj_b.shape, lambda i: (0, 0), **resident),
        ],
        out_specs=pl.BlockSpec((tm, d), lambda i: (i, 0)),
        out_shape=jax.ShapeDtypeStruct((t, d), BF16),
        compiler_params=_params(("arbitrary",)),
        name="gated_merge",
    )(o_a, o_b, u_main, u_main, proj_a, proj_b)


def _out_kernel(y_ref, x_ref, w_ref, mod_ref, nmod_ref, nw_ref, *out_refs, final):
    r = jnp.dot(y_ref[...], w_ref[...], preferred_element_type=F32)
    x_new = x_ref[...] + mod_ref[0, 2:3, :] * r
    if final:
        (o_ref,) = out_refs
        o_ref[...] = x_new * lax.rsqrt(jnp.mean(x_new * x_new, axis=-1, keepdims=True) + NORM_EPS) * nw_ref[...]
    else:
        x_out_ref, h_ref = out_refs
        x_out_ref[...] = x_new
        h_ref[...] = _modulated_norm(x_new, nw_ref[...], nmod_ref[0]).astype(h_ref.dtype)


def _out_proj(y, x2, w_out, mod, next_mod, next_nw, batch, tm, final):
    t, d = x2.shape
    nt = t // batch // tm

    def row(b, i):
        return (b * nt + i, 0)

    if final:
        out_specs = pl.BlockSpec((tm, d), row)
        out_shape = jax.ShapeDtypeStruct((t, d), F32)
    else:
        out_specs = [pl.BlockSpec((tm, d), row)] * 2
        out_shape = [jax.ShapeDtypeStruct((t, d), F32), jax.ShapeDtypeStruct((t, d), BF16)]
    return pl.pallas_call(
        functools.partial(_out_kernel, final=final),
        grid=(batch, nt),
        in_specs=[
            pl.BlockSpec((tm, d), row),
            pl.BlockSpec((tm, d), row),
            pl.BlockSpec(w_out.shape, lambda b, i: (0, 0), pipeline_mode=pl.Buffered(1)),
            pl.BlockSpec((1, 3, d), lambda b, i: (b, 0, 0)),
            pl.BlockSpec((1, 3, d), lambda b, i: (b, 0, 0)),
            pl.BlockSpec((1, d), lambda b, i: (0, 0)),
        ],
        out_specs=out_specs,
        out_shape=out_shape,
        compiler_params=_params(("arbitrary", "arbitrary")),
        name="out_proj_final" if final else "out_proj",
    )(y, x2, w_out, mod, next_mod, next_nw)


def _relayout_kernel(w_ref, main_ref, tail_ref, *, main_groups, tail_groups, pad_from):
    def move(dst_ref, src, width, dst):
        lead = src % LANES
        if lead == 0:
            dst_ref[0, :, dst:dst + width] = w_ref[0, :, src:src + width].astype(dst_ref.dtype)
        else:
            blk = w_ref[0, :, src - lead:src + width]
            dst_ref[0, :, dst:dst + width] = blk[:, lead:lead + width].astype(dst_ref.dtype)

    for src, width, dst in main_groups:
        move(main_ref, src, width, dst)
    for src, width, dst in tail_groups:
        move(tail_ref, src, width, dst)
    n_tail = tail_ref.shape[-1]
    tail_ref[0, :, pad_from:n_tail] = jnp.zeros((tail_ref.shape[1], n_tail - pad_from), tail_ref.dtype)


def _relayout_weights(w_in, main_groups, tail_groups, n_main, n_tail, pad_from, tr):
    depth, d, n_in = w_in.shape
    return pl.pallas_call(
        functools.partial(_relayout_kernel, main_groups=main_groups, tail_groups=tail_groups, pad_from=pad_from),
        grid=(depth, d // tr),
        in_specs=[pl.BlockSpec((1, tr, n_in), lambda l, i: (l, i, 0))],
        out_specs=[
            pl.BlockSpec((1, tr, n_main), lambda l, i: (l, i, 0)),
            pl.BlockSpec((1, tr, n_tail), lambda l, i: (l, i, 0)),
        ],
        out_shape=[
            jax.ShapeDtypeStruct((depth, d, n_main), BF16),
            jax.ShapeDtypeStruct((depth, d, n_tail), BF16),
        ],
        compiler_params=_params(("arbitrary", "arbitrary")),
        name="projection_weight_relayout",
    )(w_in)


def _tile(n, target, align):
    if n <= target:
        return n
    best = align
    for cand in range(align, target + 1, align):
        if n % cand == 0:
            best = cand
    assert n % best == 0
    return best


def kernel(x, c, positions, ada_w, ada_b, norm_w, w_in, conv_w, gdn_a_log, gdn_dt_bias, gdn_norm_w,
           swa_sinks, proj_a, proj_b, w_out, final_norm_w):
    batch, seq, d = x.shape
    depth = ada_w.shape[0]
    t = batch * seq
    gw = conv_w.shape[-1] // 3
    gh = gdn_a_log.shape[-1]
    sw = proj_b.shape[1]
    q_heads = swa_sinks.shape[-1]
    in_cols = w_in.shape[-1]
    kvw = (in_cols - 4 * gw - 2 * gh - 2 * sw - 2 * d) // 2
    kv_heads = kvw // SWA_HEAD_DIM
    assert gw == d and sw == d, "column-block layout assumes both branch widths equal d_model"
    assert gw == gh * GDN_HEAD_DIM and sw == q_heads * SWA_HEAD_DIM
    assert 2 * gh <= LANES and kvw % LANES == 0 and q_heads % (2 * kv_heads) == 0
    assert seq % SWA_BLOCK == 0 and seq % (4 * GDN_CHUNK) == 0 and 2 * GDN_CHUNK == LANES and gh % 2 == 0

    sizes = (3 * gw, gh, gh, gw, sw, kvw, kvw, sw, d, d)
    offs = [0]
    for s in sizes:
        offs.append(offs[-1] + s)
    blk_qb, blk_zb, blk_ga, blk_gb = 4, 5, 6, 7

    src = dict(zip(("qkv_a", "beta_a", "a_a", "z_a", "q_b", "k_b", "v_b", "z_b", "g_a", "g_b"), offs))
    main_groups = ((src["qkv_a"], 3 * gw, 0), (src["z_a"], gw, 3 * d), (src["q_b"], sw, 4 * d),
                   (src["z_b"], sw, 5 * d), (src["g_a"], d, 6 * d), (src["g_b"], d, 7 * d))
    tail_groups = ((src["k_b"], kvw, 0), (src["v_b"], kvw, kvw), (src["beta_a"], 2 * gh, 2 * kvw))
    w_main, w_tail = _relayout_weights(w_in, main_groups, tail_groups, 8 * d, 2 * kvw + LANES,
                                       2 * kvw + 2 * gh, _tile(d, 128, 16))

    head_params = jnp.zeros((depth, 8, LANES), F32)
    head_params = head_params.at[:, 0, gh:2 * gh].set(gdn_a_log.astype(F32))
    head_params = head_params.at[:, 1, gh:2 * gh].set(gdn_dt_bias.astype(F32))

    tm_big = _tile(seq, 1024, 8)
    tm_mid = _tile(seq, 512, 8)
    mod = _modulation(c, ada_w, ada_b, _tile(3 * d, 1024, LANES)).reshape(depth, batch, 3, d)
    cos_t, sin_t = _rope_tables(positions, _tile(t, 512, 8))

    x2 = x.reshape(t, d)
    h = _prep(x2, mod[0], norm_w[0:1], batch, _tile(seq, 512, 8))
    out = None
    for l in range(depth):
        u_main = _in_proj_main(h, w_main, l, tm_big, _tile(8 * d, 2048, LANES))
        kv, gb = _in_proj_tail(h, w_tail, l, 2 * kvw, tm_big)
        o_a = _gdn(u_main, gb, conv_w[l], head_params[l], gdn_norm_w[l:l + 1], batch, gh, 4 * GDN_CHUNK)
        o_b = _swa(u_main, kv, cos_t, sin_t, swa_sinks[l], batch, q_heads, kv_heads, blk_qb, blk_zb)
        y = _merge(o_a, o_b, u_main, proj_a[l].astype(BF16), proj_b[l].astype(BF16), blk_ga, blk_gb, tm_mid)
        final = l == depth - 1
        if final:
            out = _out_proj(y, x2, w_out[l].astype(BF16), mod[l], mod[l], final_norm_w.reshape(1, d), batch, tm_mid, True)
        else:
            x2, h = _out_proj(y, x2, w_out[l].astype(BF16), mod[l], mod[l + 1], norm_w[l + 1:l + 2], batch, tm_mid, False)
    return out.reshape(batch, seq, d)
```

```python
import functools

import jax
import jax.numpy as jnp
from jax import lax
from jax.experimental import pallas as pl
from jax.experimental.pallas import tpu as pltpu

F32 = jnp.float32
BF16 = jnp.bfloat16

LANES = 128
GDN_HEAD_DIM = 128
GDN_CHUNK = 64
CONV_K = 4
CONV_HALO = 16
SWA_HEAD_DIM = 64
SWA_BLOCK = 128
ROPE_THETA = 10000.0
NORM_EPS = 1e-6
L2_EPS = 1e-6
MASK_VALUE = -1e30
LOG2_E = 1.4426950408889634
VMEM_LIMIT = 56 * 1024 * 1024


def _params(semantics):
    return pltpu.CompilerParams(dimension_semantics=semantics, vmem_limit_bytes=VMEM_LIMIT)


def _mm(a, b):
    return jnp.dot(a.astype(BF16), b.astype(BF16), preferred_element_type=F32)


def _mm_nt(a, b):
    return lax.dot_general(a.astype(BF16), b.astype(BF16), (((1,), (1,)), ((), ())),
                           preferred_element_type=F32)


def _mm_tn(a, b):
    return lax.dot_general(a.astype(BF16), b.astype(BF16), (((0,), (0,)), ((), ())),
                           preferred_element_type=F32)


def _silu(x):
    return x * jax.nn.sigmoid(x)


def _softplus(x):
    return jnp.maximum(x, 0.0) + jnp.log1p(jnp.exp(-jnp.abs(x)))


def _mod_kernel(c_ref, w_ref, b_ref, o_ref):
    c = c_ref[...]
    act = _silu(c)
    hi = act.astype(BF16)
    lo = (act - hi.astype(F32)).astype(BF16)
    w = w_ref[0].astype(BF16)
    acc = jnp.dot(hi, w, preferred_element_type=F32) + jnp.dot(lo, w, preferred_element_type=F32)
    o_ref[0] = acc + b_ref[0]


def _modulation(c, ada_w, ada_b, tn):
    depth, d, n = ada_w.shape
    b = c.shape[0]
    return pl.pallas_call(
        _mod_kernel,
        grid=(depth, n // tn),
        in_specs=[
            pl.BlockSpec((b, d), lambda l, j: (0, 0)),
            pl.BlockSpec((1, d, tn), lambda l, j: (l, 0, j)),
            pl.BlockSpec((1, 1, tn), lambda l, j: (l, 0, j)),
        ],
        out_specs=pl.BlockSpec((1, b, tn), lambda l, j: (l, 0, j)),
        out_shape=jax.ShapeDtypeStruct((depth, b, n), F32),
        compiler_params=_params(("arbitrary", "arbitrary")),
        name="adaln_modulation",
    )(c, ada_w, ada_b.reshape(depth, 1, n))


def _modulated_norm(x, nw, mod):
    y = x * lax.rsqrt(jnp.mean(x * x, axis=-1, keepdims=True) + NORM_EPS) * nw
    return y * (1.0 + mod[1:2, :]) + mod[0:1, :]


def _prep_kernel(x_ref, mod_ref, nw_ref, h_ref):
    h_ref[...] = _modulated_norm(x_ref[...], nw_ref[...], mod_ref[0]).astype(h_ref.dtype)


def _prep(x2, mod, nw, batch, tm):
    t, d = x2.shape
    nt = t // batch // tm
    return pl.pallas_call(
        _prep_kernel,
        grid=(batch, nt),
        in_specs=[
            pl.BlockSpec((tm, d), lambda b, i: (b * nt + i, 0)),
            pl.BlockSpec((1, 3, d), lambda b, i: (b, 0, 0)),
            pl.BlockSpec((1, d), lambda b, i: (0, 0)),
        ],
        out_specs=pl.BlockSpec((tm, d), lambda b, i: (b * nt + i, 0)),
        out_shape=jax.ShapeDtypeStruct((t, d), BF16),
        compiler_params=_params(("arbitrary", "arbitrary")),
        name="modulated_norm",
    )(x2, mod, nw)


def _matmul_kernel(a_ref, w_ref, o_ref):
    o_ref[...] = jnp.dot(a_ref[...], w_ref[...], preferred_element_type=F32).astype(o_ref.dtype)


def _in_proj_main(h, w, layer, tm, tn):
    t, d = h.shape
    n = w.shape[-1]
    return pl.pallas_call(
        _matmul_kernel,
        grid=(n // tn, t // tm),
        in_specs=[
            pl.BlockSpec((tm, d), lambda j, i: (i, 0)),
            pl.BlockSpec((None, d, tn), lambda j, i: (layer, 0, j)),
        ],
        out_specs=pl.BlockSpec((tm, tn), lambda j, i: (i, j)),
        out_shape=jax.ShapeDtypeStruct((t, n), BF16),
        compiler_params=_params(("arbitrary", "arbitrary")),
        name="in_proj_main",
    )(h, w)


INV_BASE = 8


def _pair_iotas(n):
    ii = lax.broadcasted_iota(jnp.int32, (n, 2 * n), 0)
    jj = lax.broadcasted_iota(jnp.int32, (n, 2 * n), 1) & (n - 1)
    return ii, jj


def _inverse_masks(n):
    ii, jj = _pair_iotas(n)
    shift = INV_BASE.bit_length() - 1
    base = (ii >> shift) == (jj >> shift)
    levels = []
    blk = INV_BASE
    while blk < n:
        same_parent = (ii >> (shift + 1)) == (jj >> (shift + 1))
        levels.append(jnp.logical_and(same_parent, (ii >> shift) != (jj >> shift)))
        shift += 1
        blk *= 2
    return base, levels


def _mm_pair(xp, pp, low_half):
    blockdiag = jnp.concatenate([jnp.where(low_half, pp, 0.0), jnp.where(low_half, 0.0, pp)], axis=0)
    return _mm(xp, blockdiag)


def _unit_lower_inverse(lms, eye, masks, low_half):
    base, levels = masks
    ps = [jnp.where(base, lm, 0.0) for lm in lms]
    xs = [eye - p for p in ps]
    step = 2
    while step < INV_BASE:
        ps = [_mm_pair(p, p, low_half) for p in ps]
        xs = [x + _mm_pair(x, p, low_half) for x, p in zip(xs, ps)]
        step *= 2
    for off_mask in levels:
        ts = [_mm_pair(x, jnp.where(off_mask, lm, 0.0), low_half) for x, lm in zip(xs, lms)]
        xs = [x - _mm_pair(t, x, low_half) for x, t in zip(xs, ts)]
    return xs


def _chunk_cumsum(g):
    rows = lax.broadcasted_iota(jnp.int32, g.shape, 0)
    s = 1
    while s < g.shape[0]:
        g = g + jnp.where(rows >= s, pltpu.roll(g, s, 0), 0.0)
        s *= 2
    return g


def _gdn_kernel(q_ref, k_ref, v_ref, z_ref, h_ref, wgb_ref, cw_ref, hp_ref, nw_ref, o_ref,
                halo_ref, state_ref, *, ts, heads):
    c = GDN_CHUNK
    hd = GDN_HEAD_DIM
    w = heads * hd
    t = pl.program_id(1)

    @pl.when(t == 0)
    def _():
        halo_ref[...] = jnp.zeros(halo_ref.shape, halo_ref.dtype)
        state_ref[...] = jnp.zeros(state_ref.shape, F32)

    ii, jj = _pair_iotas(c)
    causal = ii >= jj
    strict = ii > jj
    eye = jnp.where(ii == jj, 1.0, 0.0).astype(F32)
    inv_masks = _inverse_masks(c)
    low_half = lax.broadcasted_iota(jnp.int32, (c, 2 * c), 1) < c
    zeros_hd = jnp.zeros((c, hd), F32)
    norm_w = nw_ref[...]

    n_sh = CONV_K - 1
    rr = lax.broadcasted_iota(jnp.int32, (n_sh * c, CONV_HALO + c), 0)
    cc = lax.broadcasted_iota(jnp.int32, (n_sh * c, CONV_HALO + c), 1)
    log_c = c.bit_length() - 1
    shift_mat = jnp.where(cc == (rr & (c - 1)) + (rr >> log_c) + (CONV_HALO - n_sh), 1.0, 0.0).astype(BF16)
    ones_sum = jnp.ones((hd, hd), BF16)
    ones_mean = jnp.full((hd, hd), 1.0 / hd, BF16)

    refs = (q_ref, k_ref, v_ref)
    n_chunks = ts // c
    items = [(ci, h) for ci in range(n_chunks) for h in range(heads)]
    pair_items = [(ci, h) for ci in range(n_chunks) for h in range(0, heads, 2)]
    at = {it: n for n, it in enumerate(items)}

    gb_tile = jnp.dot(h_ref[...], wgb_ref[...], preferred_element_type=F32)
    beta_all, gc_all, gc_t, egc_all, ekd_all, cdec_all = [], [], [], [], [], []
    for ci in range(n_chunks):
        gb = gb_tile[ci * c:(ci + 1) * c, :]
        beta_all.append(jax.nn.sigmoid(gb))
        gc = _chunk_cumsum(-jnp.exp(hp_ref[0:1, :]) * _softplus(gb + hp_ref[1:2, :]))
        gc_all.append(gc)
        gc_t.append(jnp.concatenate([gc, pltpu.roll(gc, LANES - 1, 1)], axis=0).T)
        glast = gc[c - 1:c, :]
        egc_all.append(jnp.exp(gc))
        ekd_all.append(jnp.exp(glast - gc))
        cdec_all.append(jnp.exp(glast))

    def window(part, ci, h):
        cols = slice(h * hd, (h + 1) * hd)
        if ci == 0:
            prev = halo_ref[:, part * w + h * hd:part * w + (h + 1) * hd]
        else:
            prev = refs[part][ci * c - CONV_HALO:ci * c, cols]
        return jnp.concatenate([prev, refs[part][ci * c:(ci + 1) * c, cols]], axis=0)

    shifted = [jnp.dot(shift_mat, jnp.concatenate([window(p, ci, h) for p in range(3)], axis=1),
                       preferred_element_type=F32) for ci, h in items]

    def conv_silu(part, ci, h):
        col = part * w + h * hd
        acc = refs[part][ci * c:(ci + 1) * c, h * hd:(h + 1) * hd].astype(F32) * cw_ref[n_sh:n_sh + 1, col:col + hd]
        for j in range(n_sh):
            acc = acc + shifted[at[ci, h]][j * c:(j + 1) * c, part * hd:(part + 1) * hd] * cw_ref[j:j + 1, col:col + hd]
        return _silu(acc)

    q = [conv_silu(0, ci, h) for ci, h in items]
    k = [conv_silu(1, ci, h) for ci, h in items]
    v = [conv_silu(2, ci, h) for ci, h in items]
    ssq = _mm(jnp.concatenate([x * x for x in q + k], axis=0), ones_sum)
    n_it = len(items)
    q = [x * lax.rsqrt(ssq[n * c:(n + 1) * c] + L2_EPS) * (hd ** -0.5) for n, x in enumerate(q)]
    k = [x * lax.rsqrt(ssq[(n_it + n) * c:(n_it + n + 1) * c] + L2_EPS) for n, x in enumerate(k)]

    bcol = [beta_all[ci][:, h:h + 1] for ci, h in items]
    gcol = [gc_all[ci][:, heads + h:heads + h + 1] for ci, h in items]
    egcol = [egc_all[ci][:, heads + h:heads + h + 1] for ci, h in items]
    ekdcol = [ekd_all[ci][:, heads + h:heads + h + 1] for ci, h in items]
    cdec = [cdec_all[ci][:, heads + h:heads + h + 1] for ci, h in items]
    kb = [kk * b for kk, b in zip(k, bcol)]
    rhs = [jnp.concatenate([vv * b, kk * e], axis=1) for vv, b, kk, e in zip(v, bcol, kb, egcol)]

    decay = [jnp.where(causal, jnp.exp(jnp.where(causal, jnp.where(low_half, gcol[at[ci, h]], gcol[at[ci, h + 1]])
                                                 - gc_t[ci][heads + h:heads + h + 1, :], 0.0)), 0.0)
             for ci, h in pair_items]
    both = [_mm_nt(jnp.concatenate([jnp.concatenate([kb[at[ci, h]], kb[at[ci, h + 1]]], axis=1),
                                    jnp.concatenate([q[at[ci, h]], q[at[ci, h + 1]]], axis=1)], axis=0),
                   jnp.concatenate([jnp.concatenate([k[at[ci, h]], zeros_hd], axis=1),
                                    jnp.concatenate([zeros_hd, k[at[ci, h + 1]]], axis=1)], axis=0))
            for ci, h in pair_items]
    lm = [jnp.where(strict, b[0:c] * d, 0.0) for b, d in zip(both, decay)]
    qkm_p = [b[c:2 * c] * d for b, d in zip(both, decay)]
    tinv = _unit_lower_inverse(lm, eye, inv_masks, low_half)
    sol_p = [_mm(jnp.concatenate([jnp.where(low_half, tp, 0.0), jnp.where(low_half, 0.0, tp)], axis=0),
                 jnp.concatenate([rhs[at[ci, h]], rhs[at[ci, h + 1]]], axis=0))
             for tp, (ci, h) in zip(tinv, pair_items)]
    q_dec = [qq * e for qq, e in zip(q, egcol)]
    k_dec = [kk * e for kk, e in zip(k, ekdcol)]

    hs = range(heads)
    pairs = range(0, heads, 2)
    for ci in range(n_chunks):
        sol = [sol_p[(ci * heads + h) // 2][(h % 2) * c:(h % 2 + 1) * c] for h in hs]
        s_old = [state_ref[h] for h in hs]
        ws_qs = [_mm(jnp.concatenate([sol[h][:, hd:2 * hd], q_dec[at[ci, h]]], axis=0), s_old[h]) for h in hs]
        v_new = [sol[h][:, 0:hd] - ws_qs[h][0:c] for h in hs]
        o_inter = [ws_qs[h][c:2 * c] for h in hs]
        o_intra_p = [_mm(jnp.concatenate([jnp.where(low_half, qkm_p[(ci * heads + h) // 2], 0.0),
                                          jnp.where(low_half, 0.0, qkm_p[(ci * heads + h) // 2])], axis=0),
                         jnp.concatenate([v_new[h], v_new[h + 1]], axis=0)) for h in pairs]
        o = [o_inter[h] + o_intra_p[h // 2][(h % 2) * c:(h % 2 + 1) * c] for h in hs]
        for h in hs:
            state_ref[h] = s_old[h] * cdec[at[ci, h]] + _mm_tn(k_dec[at[ci, h]], v_new[h])
        msq = _mm(jnp.concatenate([x * x for x in o], axis=0), ones_mean)
        for h in hs:
            on = o[h] * lax.rsqrt(msq[h * c:(h + 1) * c] + NORM_EPS) * norm_w
            z = z_ref[ci * c:(ci + 1) * c, h * hd:(h + 1) * hd].astype(F32)
            o_ref[ci * c:(ci + 1) * c, h * hd:(h + 1) * hd] = (on * _silu(z)).astype(o_ref.dtype)

    for part in range(3):
        halo_ref[:, part * w:(part + 1) * w] = refs[part][ts - CONV_HALO:ts, :]


def _gdn(u_main, h, w_gb, layer, conv_w, head_params, norm_w, batch, heads, ts):
    t, d = h.shape
    w = heads * GDN_HEAD_DIM
    nt = t // batch // ts

    def col(blk):
        return pl.BlockSpec((ts, w), lambda b, i, blk=blk: (b * nt + i, blk))

    return pl.pallas_call(
        functools.partial(_gdn_kernel, ts=ts, heads=heads),
        grid=(batch, nt),
        in_specs=[
            col(0), col(1), col(2), col(3),
            pl.BlockSpec((ts, d), lambda b, i: (b * nt + i, 0)),
            pl.BlockSpec((None, d, LANES), lambda b, i: (layer, 0, 0)),
            pl.BlockSpec((CONV_K, 3 * w), lambda b, i: (0, 0)),
            pl.BlockSpec((8, LANES), lambda b, i: (0, 0)),
            pl.BlockSpec((1, GDN_HEAD_DIM), lambda b, i: (0, 0)),
        ],
        out_specs=pl.BlockSpec((ts, w), lambda b, i: (b * nt + i, 0)),
        out_shape=jax.ShapeDtypeStruct((t, w), BF16),
        scratch_shapes=[
            pltpu.VMEM((CONV_HALO, 3 * w), BF16),
            pltpu.VMEM((heads, GDN_HEAD_DIM, GDN_HEAD_DIM), F32),
        ],
        compiler_params=_params(("arbitrary", "arbitrary")),
        name="gated_deltanet",
    )(u_main, u_main, u_main, u_main, h, w_gb, conv_w, head_params, norm_w)


def _rope_table_kernel(pos_ref, invf_ref, sign_ref, cos_ref, sin_ref):
    ang = pos_ref[...].astype(F32) * invf_ref[...]
    cos_ref[...] = jnp.cos(ang)
    sin_ref[...] = jnp.sin(ang) * sign_ref[...]


def _rope_tables(positions, tr):
    t = positions.size
    half = SWA_HEAD_DIM // 2
    inv_freq = ROPE_THETA ** (-jnp.arange(half, dtype=F32) / half)
    invf = jnp.tile(inv_freq, LANES // half).reshape(1, LANES)
    sign = jnp.tile(jnp.concatenate([-jnp.ones((half,), F32), jnp.ones((half,), F32)]),
                    LANES // SWA_HEAD_DIM).reshape(1, LANES)
    return pl.pallas_call(
        _rope_table_kernel,
        grid=(t // tr,),
        in_specs=[
            pl.BlockSpec((tr, 1), lambda i: (i, 0)),
            pl.BlockSpec((1, LANES), lambda i: (0, 0)),
            pl.BlockSpec((1, LANES), lambda i: (0, 0)),
        ],
        out_specs=[pl.BlockSpec((tr, LANES), lambda i: (i, 0))] * 2,
        out_shape=[jax.ShapeDtypeStruct((t, LANES), F32)] * 2,
        compiler_params=_params(("arbitrary",)),
        name="rope_tables",
    )(positions.reshape(t, 1), invf, sign)


def _swa_kernel(sink_ref, q_ref, z_ref, h_ref, wkv_ref, cos_ref, sin_ref, o_ref, kprev_ref, vprev_ref,
                *, q_heads, kv_heads):
    blk = pl.program_id(1)
    tq = SWA_BLOCK
    hd = SWA_HEAD_DIM
    group = q_heads // kv_heads
    kvw = kv_heads * hd

    @pl.when(blk == 0)
    def _():
        kprev_ref[...] = jnp.zeros(kprev_ref.shape, kprev_ref.dtype)
        vprev_ref[...] = jnp.zeros(vprev_ref.shape, vprev_ref.dtype)

    cos = cos_ref[...]
    sin = sin_ref[...]
    kv = jnp.dot(h_ref[...], wkv_ref[...], preferred_element_type=F32)
    lane = lax.broadcasted_iota(jnp.int32, (tq, LANES), 1)
    low_half = lane < hd
    first_quarter = (lane & (hd - 1)) < (hd // 2)

    def rope(xs, cos_t, sin_t):
        partner = jnp.where(first_quarter, pltpu.roll(xs, LANES - hd // 2, 1), pltpu.roll(xs, hd // 2, 1))
        return xs * cos_t + partner * sin_t

    q_scale = (hd ** -0.5) * LOG2_E
    cos_q = cos * q_scale
    sin_q = sin * q_scale

    def dup(xs, head_in_slab):
        swapped = pltpu.roll(xs, hd, 1)
        return jnp.where(low_half, xs, swapped) if head_in_slab == 0 else jnp.where(low_half, swapped, xs)

    qi = lax.broadcasted_iota(jnp.int32, (tq, 2 * tq), 0)
    kj = lax.broadcasted_iota(jnp.int32, (tq, 2 * tq), 1)
    lowest = jnp.maximum(qi, jnp.where(blk > 0, -1, tq - 1))
    valid = jnp.logical_and(kj > lowest, kj <= qi + tq)

    kv_per_slab = LANES // hd

    def scores_of_group(g):
        slab = g // kv_per_slab
        k_slab = rope(kv[:, slab * LANES:(slab + 1) * LANES], cos, sin)
        v_slab = kv[:, kvw + slab * LANES:kvw + (slab + 1) * LANES]
        k_cur = dup(k_slab, g % kv_per_slab).astype(BF16)
        v_cur = dup(v_slab, g % kv_per_slab).astype(BF16)
        k_all = jnp.concatenate([kprev_ref[g], k_cur], axis=0)
        v_all = jnp.concatenate([vprev_ref[g], v_cur], axis=0)
        q_parts = []
        for qs in range(g * group // 2, (g + 1) * group // 2):
            q_slab = rope(q_ref[:, qs * LANES:(qs + 1) * LANES].astype(F32), cos_q, sin_q)
            q_parts.append(jnp.where(low_half, q_slab, 0.0).astype(BF16))
            q_parts.append(jnp.where(low_half, 0.0, q_slab).astype(BF16))
        scores = _mm_nt(jnp.concatenate(q_parts, axis=0), k_all)
        return k_cur, v_cur, v_all, scores

    def finish_group(g, k_cur, v_cur, v_all, scores):
        p_parts, inv_denoms = [], []
        for hh in range(group):
            sink = sink_ref[g * group + hh] * LOG2_E
            s = jnp.where(valid, scores[hh * tq:(hh + 1) * tq], MASK_VALUE)
            m = jnp.maximum(jnp.max(s, axis=-1, keepdims=True), sink)
            p = jnp.exp2(s - m)
            denom = jnp.sum(p, axis=-1, keepdims=True) + jnp.exp2(sink - m)
            p_parts.append(p.astype(BF16))
            inv_denoms.append(1.0 / denom)
        pv = jnp.dot(jnp.concatenate(p_parts, axis=0), v_all, preferred_element_type=F32)
        for hh in range(0, group, 2):
            qs = (g * group + hh) // 2
            o_even = pv[hh * tq:(hh + 1) * tq] * inv_denoms[hh]
            o_odd = pv[(hh + 1) * tq:(hh + 2) * tq] * inv_denoms[hh + 1]
            z = z_ref[:, qs * LANES:(qs + 1) * LANES].astype(F32)
            o_ref[:, qs * LANES:(qs + 1) * LANES] = (jnp.where(low_half, o_even, o_odd) * _silu(z)).astype(o_ref.dtype)
        kprev_ref[g] = k_cur
        vprev_ref[g] = v_cur

    pending = scores_of_group(0)
    for g in range(kv_heads):
        current = pending
        if g + 1 < kv_heads:
            pending = scores_of_group(g + 1)
        finish_group(g, *current)


def _swa(u_main, h, w_kv, layer, cos_t, sin_t, sinks, batch, q_heads, kv_heads, q_blk, z_blk):
    t, d = h.shape
    w = q_heads * SWA_HEAD_DIM
    nb = t // batch // SWA_BLOCK

    def row(b, i):
        return b * nb + i

    return pl.pallas_call(
        functools.partial(_swa_kernel, q_heads=q_heads, kv_heads=kv_heads),
        grid=(batch, nb),
        in_specs=[
            pl.BlockSpec(memory_space=pltpu.SMEM),
            pl.BlockSpec((SWA_BLOCK, w), lambda b, i: (row(b, i), q_blk)),
            pl.BlockSpec((SWA_BLOCK, w), lambda b, i: (row(b, i), z_blk)),
            pl.BlockSpec((SWA_BLOCK, d), lambda b, i: (row(b, i), 0)),
            pl.BlockSpec((None, d, w_kv.shape[-1]), lambda b, i: (layer, 0, 0)),
            pl.BlockSpec((SWA_BLOCK, LANES), lambda b, i: (row(b, i), 0)),
            pl.BlockSpec((SWA_BLOCK, LANES), lambda b, i: (row(b, i), 0)),
        ],
        out_specs=pl.BlockSpec((SWA_BLOCK, w), lambda b, i: (row(b, i), 0)),
        out_shape=jax.ShapeDtypeStruct((t, w), BF16),
        scratch_shapes=[
            pltpu.VMEM((kv_heads, SWA_BLOCK, LANES), BF16),
            pltpu.VMEM((kv_heads, SWA_BLOCK, LANES), BF16),
        ],
        compiler_params=_params(("arbitrary", "arbitrary")),
        name="sliding_window_attention",
    )(sinks, u_main, u_main, h, w_kv, cos_t, sin_t)


def _merge_kernel(oa_ref, ob_ref, ga_ref, gb_ref, pa_ref, pb_ref, y_ref):
    ya = jnp.dot(oa_ref[...], pa_ref[...], preferred_element_type=F32)
    yb = jnp.dot(ob_ref[...], pb_ref[...], preferred_element_type=F32)
    y = jax.nn.sigmoid(ga_ref[...].astype(F32)) * ya + jax.nn.sigmoid(gb_ref[...].astype(F32)) * yb
    y_ref[...] = y.astype(y_ref.dtype)


def _merge(o_a, o_b, u_main, proj_a, proj_b, ga_blk, gb_blk, tm):
    t, d = o_a.shape
    resident = dict(pipeline_mode=pl.Buffered(1))
    return pl.pallas_call(
        _merge_kernel,
        grid=(t // tm,),
        in_specs=[
            pl.BlockSpec((tm, d), lambda i: (i, 0)),
            pl.BlockSpec((tm, d), lambda i: (i, 0)),
            pl.BlockSpec((tm, d), lambda i: (i, ga_blk)),
            pl.BlockSpec((tm, d), lambda i: (i, gb_blk)),
            pl.BlockSpec(proj_a.shape, lambda i: (0, 0), **resident),
            pl.BlockSpec(proj_b.shape, lambda i: (0, 0), **resident),
        ],
        out_specs=pl.BlockSpec((tm, d), lambda i: (i, 0)),
        out_shape=jax.ShapeDtypeStruct((t, d), BF16),
        compiler_params=_params(("arbitrary",)),
        name="gated_merge",
    )(o_a, o_b, u_main, u_main, proj_a, proj_b)


def _out_kernel(y_ref, x_ref, w_ref, mod_ref, nmod_ref, nw_ref, *out_refs, final):
    r = jnp.dot(y_ref[...], w_ref[...], preferred_element_type=F32)
    x_new = x_ref[...] + mod_ref[0, 2:3, :] * r
    if final:
        (o_ref,) = out_refs
        o_ref[...] = x_new * lax.rsqrt(jnp.mean(x_new * x_new, axis=-1, keepdims=True) + NORM_EPS) * nw_ref[...]
    else:
        x_out_ref, h_ref = out_refs
        x_out_ref[...] = x_new
        h_ref[...] = _modulated_norm(x_new, nw_ref[...], nmod_ref[0]).astype(h_ref.dtype)


def _out_proj(y, x2, w_out, mod, next_mod, next_nw, batch, tm, final):
    t, d = x2.shape
    nt = t // batch // tm

    def row(b, i):
        return (b * nt + i, 0)

    if final:
        out_specs = pl.BlockSpec((tm, d), row)
        out_shape = jax.ShapeDtypeStruct((t, d), F32)
    else:
        out_specs = [pl.BlockSpec((tm, d), row)] * 2
        out_shape = [jax.ShapeDtypeStruct((t, d), F32), jax.ShapeDtypeStruct((t, d), BF16)]
    return pl.pallas_call(
        functools.partial(_out_kernel, final=final),
        grid=(batch, nt),
        in_specs=[
            pl.BlockSpec((tm, d), row),
            pl.BlockSpec((tm, d), row),
            pl.BlockSpec(w_out.shape, lambda b, i: (0, 0), pipeline_mode=pl.Buffered(1)),
            pl.BlockSpec((1, 3, d), lambda b, i: (b, 0, 0)),
            pl.BlockSpec((1, 3, d), lambda b, i: (b, 0, 0)),
            pl.BlockSpec((1, d), lambda b, i: (0, 0)),
        ],
        out_specs=out_specs,
        out_shape=out_shape,
        compiler_params=_params(("arbitrary", "arbitrary")),
        name="out_proj_final" if final else "out_proj",
    )(y, x2, w_out, mod, next_mod, next_nw)


ROW_ALIGN = 8


def _transpose_kernel(rows_ref, w_ref, o_ref, *, keep):
    del rows_ref
    wt = w_ref[0].T
    if keep < wt.shape[1]:
        col = lax.broadcasted_iota(jnp.int32, wt.shape, 1)
        wt = jnp.where(col < keep, wt, 0.0)
    o_ref[...] = wt.astype(o_ref.dtype)


def _gather_transposed(w_t, src_rows, tc, keep=None):
    depth, _, d = w_t.shape
    n_tiles = len(src_rows)
    grid_spec = pltpu.PrefetchScalarGridSpec(
        num_scalar_prefetch=1,
        grid=(depth, n_tiles),
        in_specs=[pl.BlockSpec((pl.Element(1), pl.Element(tc), pl.Element(d)), lambda l, j, rows: (l, rows[j] * ROW_ALIGN, 0))],
        out_specs=pl.BlockSpec((None, d, tc), lambda l, j, rows: (l, 0, j)),
    )
    return pl.pallas_call(
        functools.partial(_transpose_kernel, keep=tc if keep is None else keep),
        grid_spec=grid_spec,
        out_shape=jax.ShapeDtypeStruct((depth, d, n_tiles * tc), BF16),
        compiler_params=_params(("arbitrary", "arbitrary")),
        name="projection_weight_gather",
    )(jnp.asarray([r // ROW_ALIGN for r in src_rows], jnp.int32), w_t)


def _tile(n, target, align):
    if n <= target:
        return n
    best = align
    for cand in range(align, target + 1, align):
        if n % cand == 0:
            best = cand
    assert n % best == 0
    return best


def kernel(x, c, positions, ada_w, ada_b, norm_w, w_in, conv_w, gdn_a_log, gdn_dt_bias, gdn_norm_w,
           swa_sinks, proj_a, proj_b, w_out, final_norm_w):
    batch, seq, d = x.shape
    depth = ada_w.shape[0]
    t = batch * seq
    gw = conv_w.shape[-1] // 3
    gh = gdn_a_log.shape[-1]
    sw = proj_b.shape[1]
    q_heads = swa_sinks.shape[-1]
    in_cols = w_in.shape[-1]
    kvw = (in_cols - 4 * gw - 2 * gh - 2 * sw - 2 * d) // 2
    kv_heads = kvw // SWA_HEAD_DIM
    assert gw == d and sw == d, "column-block layout assumes both branch widths equal d_model"
    assert gw == gh * GDN_HEAD_DIM and sw == q_heads * SWA_HEAD_DIM
    assert 2 * gh <= LANES and kvw % LANES == 0 and q_heads % (2 * kv_heads) == 0
    assert seq % SWA_BLOCK == 0 and seq % (4 * GDN_CHUNK) == 0 and 2 * GDN_CHUNK == LANES and gh % 2 == 0

    sizes = (3 * gw, gh, gh, gw, sw, kvw, kvw, sw, d, d)
    offs = [0]
    for s in sizes:
        offs.append(offs[-1] + s)
    blk_qb, blk_zb, blk_ga, blk_gb = 4, 5, 6, 7

    src = dict(zip(("qkv_a", "beta_a", "a_a", "z_a", "q_b", "k_b", "v_b", "z_b", "g_a", "g_b"), offs))
    tc = _tile(kvw, 256, LANES)
    assert d % tc == 0 and all(src[name] % ROW_ALIGN == 0 for name in src if name != "a_a")
    main_src = ([src["qkv_a"] + i for i in range(0, 3 * gw, tc)]
                + [src[name] + i for name in ("z_a", "q_b", "z_b", "g_a", "g_b") for i in range(0, d, tc)])
    kv_src = [src["k_b"] + i for i in range(0, kvw, tc)] + [src["v_b"] + i for i in range(0, kvw, tc)]
    w_t = jnp.swapaxes(w_in, 1, 2)
    w_main = _gather_transposed(w_t, main_src, tc)
    w_kv = _gather_transposed(w_t, kv_src, tc)
    w_gb = _gather_transposed(w_t, [src["beta_a"]], LANES, keep=2 * gh)

    head_params = jnp.zeros((depth, 8, LANES), F32)
    head_params = head_params.at[:, 0, gh:2 * gh].set(gdn_a_log.astype(F32))
    head_params = head_params.at[:, 1, gh:2 * gh].set(gdn_dt_bias.astype(F32))

    tm_big = _tile(seq, 1024, 8)
    tm_mid = _tile(seq, 512, 8)
    mod = _modulation(c, ada_w, ada_b, _tile(3 * d, 1024, LANES)).reshape(depth, batch, 3, d)
    cos_t, sin_t = _rope_tables(positions, _tile(t, 512, 8))

    x2 = x.reshape(t, d)
    h = _prep(x2, mod[0], norm_w[0:1], batch, _tile(seq, 512, 8))
    out = None
    for l in range(depth):
        u_main = _in_proj_main(h, w_main, l, tm_big, _tile(8 * d, 2048, LANES))
        o_a = _gdn(u_main, h, w_gb, l, conv_w[l], head_params[l], gdn_norm_w[l:l + 1], batch, gh, 4 * GDN_CHUNK)
        o_b = _swa(u_main, h, w_kv, l, cos_t, sin_t, swa_sinks[l], batch, q_heads, kv_heads, blk_qb, blk_zb)
        y = _merge(o_a, o_b, u_main, proj_a[l].astype(BF16), proj_b[l].astype(BF16), blk_ga, blk_gb, tm_mid)
        final = l == depth - 1
        if final:
            out = _out_proj(y, x2, w_out[l].astype(BF16), mod[l], mod[l], final_norm_w.reshape(1, d), batch, tm_mid, True)
        else:
            x2, h = _out_proj(y, x2, w_out[l].astype(BF16), mod[l], mod[l + 1], norm_w[l + 1:l + 2], batch, tm_mid, False)
    return out.reshape(batch, seq, d)
```

```python
import functools

import jax
import jax.numpy as jnp
from jax import lax
from jax.experimental import pallas as pl
from jax.experimental.pallas import tpu as pltpu

F32 = jnp.float32
BF16 = jnp.bfloat16

LANES = 128
GDN_HEAD_DIM = 128
GDN_CHUNK = 64
CONV_K = 4
CONV_HALO = 16
SWA_HEAD_DIM = 64
SWA_BLOCK = 128
ROPE_THETA = 10000.0
NORM_EPS = 1e-6
L2_EPS = 1e-6
MASK_VALUE = -1e30
LOG2_E = 1.4426950408889634
VMEM_LIMIT = 56 * 1024 * 1024


def _params(semantics):
    return pltpu.CompilerParams(dimension_semantics=semantics, vmem_limit_bytes=VMEM_LIMIT)


def _mm(a, b):
    return jnp.dot(a.astype(BF16), b.astype(BF16), preferred_element_type=F32)


def _mm_nt(a, b):
    return lax.dot_general(a.astype(BF16), b.astype(BF16), (((1,), (1,)), ((), ())),
                           preferred_element_type=F32)


def _mm_tn(a, b):
    return lax.dot_general(a.astype(BF16), b.astype(BF16), (((0,), (0,)), ((), ())),
                           preferred_element_type=F32)


def _silu(x):
    return x * jax.nn.sigmoid(x)


def _softplus(x):
    return jnp.maximum(x, 0.0) + jnp.log1p(jnp.exp(-jnp.abs(x)))


def _mod_kernel(c_ref, w_ref, b_ref, o_ref):
    c = c_ref[...]
    act = _silu(c)
    hi = act.astype(BF16)
    lo = (act - hi.astype(F32)).astype(BF16)
    w = w_ref[0].astype(BF16)
    acc = jnp.dot(hi, w, preferred_element_type=F32) + jnp.dot(lo, w, preferred_element_type=F32)
    o_ref[0] = acc + b_ref[0]


def _modulation(c, ada_w, ada_b, tn):
    depth, d, n = ada_w.shape
    b = c.shape[0]
    return pl.pallas_call(
        _mod_kernel,
        grid=(depth, n // tn),
        in_specs=[
            pl.BlockSpec((b, d), lambda l, j: (0, 0)),
            pl.BlockSpec((1, d, tn), lambda l, j: (l, 0, j)),
            pl.BlockSpec((1, 1, tn), lambda l, j: (l, 0, j)),
        ],
        out_specs=pl.BlockSpec((1, b, tn), lambda l, j: (l, 0, j)),
        out_shape=jax.ShapeDtypeStruct((depth, b, n), F32),
        compiler_params=_params(("arbitrary", "arbitrary")),
        name="adaln_modulation",
    )(c, ada_w, ada_b.reshape(depth, 1, n))


def _modulated_norm(x, nw, mod):
    y = x * lax.rsqrt(jnp.mean(x * x, axis=-1, keepdims=True) + NORM_EPS) * nw
    return y * (1.0 + mod[1:2, :]) + mod[0:1, :]


def _prep_kernel(x_ref, mod_ref, nw_ref, h_ref):
    h_ref[...] = _modulated_norm(x_ref[...], nw_ref[...], mod_ref[0]).astype(h_ref.dtype)


def _prep(x2, mod, nw, batch, tm):
    t, d = x2.shape
    nt = t // batch // tm
    return pl.pallas_call(
        _prep_kernel,
        grid=(batch, nt),
        in_specs=[
            pl.BlockSpec((tm, d), lambda b, i: (b * nt + i, 0)),
            pl.BlockSpec((1, 3, d), lambda b, i: (b, 0, 0)),
            pl.BlockSpec((1, d), lambda b, i: (0, 0)),
        ],
        out_specs=pl.BlockSpec((tm, d), lambda b, i: (b * nt + i, 0)),
        out_shape=jax.ShapeDtypeStruct((t, d), BF16),
        compiler_params=_params(("arbitrary", "arbitrary")),
        name="modulated_norm",
    )(x2, mod, nw)


def _matmul_kernel(a_ref, w_ref, o_ref):
    o_ref[...] = jnp.dot(a_ref[...], w_ref[...], preferred_element_type=F32).astype(o_ref.dtype)


def _in_proj_main(h, w, layer, tm, tn):
    t, d = h.shape
    n = w.shape[-1]
    return pl.pallas_call(
        _matmul_kernel,
        grid=(n // tn, t // tm),
        in_specs=[
            pl.BlockSpec((tm, d), lambda j, i: (i, 0)),
            pl.BlockSpec((None, d, tn), lambda j, i: (layer, 0, j)),
        ],
        out_specs=pl.BlockSpec((tm, tn), lambda j, i: (i, j)),
        out_shape=jax.ShapeDtypeStruct((t, n), BF16),
        compiler_params=_params(("arbitrary", "arbitrary")),
        name="in_proj_main",
    )(h, w)


INV_BASE = 8


def _pair_iotas(n):
    ii = lax.broadcasted_iota(jnp.int32, (n, 2 * n), 0)
    jj = lax.broadcasted_iota(jnp.int32, (n, 2 * n), 1) & (n - 1)
    return ii, jj


def _inverse_masks(n):
    ii, jj = _pair_iotas(n)
    shift = INV_BASE.bit_length() - 1
    base = (ii >> shift) == (jj >> shift)
    levels = []
    blk = INV_BASE
    while blk < n:
        same_parent = (ii >> (shift + 1)) == (jj >> (shift + 1))
        levels.append(jnp.logical_and(same_parent, (ii >> shift) != (jj >> shift)))
        shift += 1
        blk *= 2
    return base, levels


def _mm_pair(xp, pp, low_half):
    blockdiag = jnp.concatenate([jnp.where(low_half, pp, 0.0), jnp.where(low_half, 0.0, pp)], axis=0)
    return _mm(xp, blockdiag)


def _unit_lower_inverse(lms, eye, masks, low_half):
    base, levels = masks
    ps = [jnp.where(base, lm, 0.0) for lm in lms]
    xs = [eye - p for p in ps]
    step = 2
    while step < INV_BASE:
        ps = [_mm_pair(p, p, low_half) for p in ps]
        xs = [x + _mm_pair(x, p, low_half) for x, p in zip(xs, ps)]
        step *= 2
    for off_mask in levels:
        ts = [_mm_pair(x, jnp.where(off_mask, lm, 0.0), low_half) for x, lm in zip(xs, lms)]
        xs = [x - _mm_pair(t, x, low_half) for x, t in zip(xs, ts)]
    return xs


def _chunk_cumsum(g):
    rows = lax.broadcasted_iota(jnp.int32, g.shape, 0)
    s = 1
    while s < g.shape[0]:
        g = g + jnp.where(rows >= s, pltpu.roll(g, s, 0), 0.0)
        s *= 2
    return g


def _gdn_kernel(q_ref, k_ref, v_ref, z_ref, gb_ref, cw_ref, hp_ref, nw_ref, o_ref,
                halo_ref, state_ref, *, ts, heads):
    c = GDN_CHUNK
    hd = GDN_HEAD_DIM
    w = heads * hd
    t = pl.program_id(1)

    @pl.when(t == 0)
    def _():
        halo_ref[...] = jnp.zeros(halo_ref.shape, halo_ref.dtype)
        state_ref[...] = jnp.zeros(state_ref.shape, F32)

    ii, jj = _pair_iotas(c)
    causal = ii >= jj
    strict = ii > jj
    eye = jnp.where(ii == jj, 1.0, 0.0).astype(F32)
    inv_masks = _inverse_masks(c)
    low_half = lax.broadcasted_iota(jnp.int32, (c, 2 * c), 1) < c
    zeros_hd = jnp.zeros((c, hd), F32)
    norm_w = nw_ref[...]

    n_sh = CONV_K - 1
    rr = lax.broadcasted_iota(jnp.int32, (n_sh * c, CONV_HALO + c), 0)
    cc = lax.broadcasted_iota(jnp.int32, (n_sh * c, CONV_HALO + c), 1)
    log_c = c.bit_length() - 1
    shift_mat = jnp.where(cc == (rr & (c - 1)) + (rr >> log_c) + (CONV_HALO - n_sh), 1.0, 0.0).astype(BF16)
    ones_sum = jnp.ones((hd, hd), BF16)
    ones_mean = jnp.full((hd, hd), 1.0 / hd, BF16)

    refs = (q_ref, k_ref, v_ref)
    n_chunks = ts // c
    items = [(ci, h) for ci in range(n_chunks) for h in range(heads)]
    pair_items = [(ci, h) for ci in range(n_chunks) for h in range(0, heads, 2)]
    at = {it: n for n, it in enumerate(items)}

    beta_all, gc_all, gc_t, egc_all, ekd_all, cdec_all = [], [], [], [], [], []
    for ci in range(n_chunks):
        gb = gb_ref[ci * c:(ci + 1) * c, :]
        beta_all.append(jax.nn.sigmoid(gb))
        gc = _chunk_cumsum(-jnp.exp(hp_ref[0:1, :]) * _softplus(gb + hp_ref[1:2, :]))
        gc_all.append(gc)
        gc_t.append(jnp.concatenate([gc, pltpu.roll(gc, LANES - 1, 1)], axis=0).T)
        glast = gc[c - 1:c, :]
        egc_all.append(jnp.exp(gc))
        ekd_all.append(jnp.exp(glast - gc))
        cdec_all.append(jnp.exp(glast))

    def window(part, ci, h):
        cols = slice(h * hd, (h + 1) * hd)
        if ci == 0:
            prev = halo_ref[:, part * w + h * hd:part * w + (h + 1) * hd]
        else:
            prev = refs[part][ci * c - CONV_HALO:ci * c, cols]
        return jnp.concatenate([prev, refs[part][ci * c:(ci + 1) * c, cols]], axis=0)

    shifted = [jnp.dot(shift_mat, jnp.concatenate([window(p, ci, h) for p in range(3)], axis=1),
                       preferred_element_type=F32) for ci, h in items]

    def conv_silu(part, ci, h):
        col = part * w + h * hd
        acc = refs[part][ci * c:(ci + 1) * c, h * hd:(h + 1) * hd].astype(F32) * cw_ref[n_sh:n_sh + 1, col:col + hd]
        for j in range(n_sh):
            acc = acc + shifted[at[ci, h]][j * c:(j + 1) * c, part * hd:(part + 1) * hd] * cw_ref[j:j + 1, col:col + hd]
        return _silu(acc)

    q = [conv_silu(0, ci, h) for ci, h in items]
    k = [conv_silu(1, ci, h) for ci, h in items]
    v = [conv_silu(2, ci, h) for ci, h in items]
    ssq = _mm(jnp.concatenate([x * x for x in q + k], axis=0), ones_sum)
    n_it = len(items)
    q = [x * lax.rsqrt(ssq[n * c:(n + 1) * c] + L2_EPS) * (hd ** -0.5) for n, x in enumerate(q)]
    k = [x * lax.rsqrt(ssq[(n_it + n) * c:(n_it + n + 1) * c] + L2_EPS) for n, x in enumerate(k)]

    bcol = [beta_all[ci][:, h:h + 1] for ci, h in items]
    gcol = [gc_all[ci][:, heads + h:heads + h + 1] for ci, h in items]
    egcol = [egc_all[ci][:, heads + h:heads + h + 1] for ci, h in items]
    ekdcol = [ekd_all[ci][:, heads + h:heads + h + 1] for ci, h in items]
    cdec = [cdec_all[ci][:, heads + h:heads + h + 1] for ci, h in items]
    kb = [kk * b for kk, b in zip(k, bcol)]
    rhs = [jnp.concatenate([vv * b, kk * e], axis=1) for vv, b, kk, e in zip(v, bcol, kb, egcol)]

    decay = [jnp.where(causal, jnp.exp(jnp.where(causal, jnp.where(low_half, gcol[at[ci, h]], gcol[at[ci, h + 1]])
                                                 - gc_t[ci][heads + h:heads + h + 1, :], 0.0)), 0.0)
             for ci, h in pair_items]
    both = [_mm_nt(jnp.concatenate([jnp.concatenate([kb[at[ci, h]], kb[at[ci, h + 1]]], axis=1),
                                    jnp.concatenate([q[at[ci, h]], q[at[ci, h + 1]]], axis=1)], axis=0),
                   jnp.concatenate([jnp.concatenate([k[at[ci, h]], zeros_hd], axis=1),
                                    jnp.concatenate([zeros_hd, k[at[ci, h + 1]]], axis=1)], axis=0))
            for ci, h in pair_items]
    lm = [jnp.where(strict, b[0:c] * d, 0.0) for b, d in zip(both, decay)]
    qkm_p = [b[c:2 * c] * d for b, d in zip(both, decay)]
    tinv = _unit_lower_inverse(lm, eye, inv_masks, low_half)
    sol_p = [_mm(jnp.concatenate([jnp.where(low_half, tp, 0.0), jnp.where(low_half, 0.0, tp)], axis=0),
                 jnp.concatenate([rhs[at[ci, h]], rhs[at[ci, h + 1]]], axis=0))
             for tp, (ci, h) in zip(tinv, pair_items)]
    q_dec = [qq * e for qq, e in zip(q, egcol)]
    k_dec = [kk * e for kk, e in zip(k, ekdcol)]

    hs = range(heads)
    pairs = range(0, heads, 2)
    for ci in range(n_chunks):
        sol = [sol_p[(ci * heads + h) // 2][(h % 2) * c:(h % 2 + 1) * c] for h in hs]
        s_old = [state_ref[h] for h in hs]
        ws_qs = [_mm(jnp.concatenate([sol[h][:, hd:2 * hd], q_dec[at[ci, h]]], axis=0), s_old[h]) for h in hs]
        v_new = [sol[h][:, 0:hd] - ws_qs[h][0:c] for h in hs]
        o_inter = [ws_qs[h][c:2 * c] for h in hs]
        o_intra_p = [_mm(jnp.concatenate([jnp.where(low_half, qkm_p[(ci * heads + h) // 2], 0.0),
                                          jnp.where(low_half, 0.0, qkm_p[(ci * heads + h) // 2])], axis=0),
                         jnp.concatenate([v_new[h], v_new[h + 1]], axis=0)) for h in pairs]
        o = [o_inter[h] + o_intra_p[h // 2][(h % 2) * c:(h % 2 + 1) * c] for h in hs]
        for h in hs:
            state_ref[h] = s_old[h] * cdec[at[ci, h]] + _mm_tn(k_dec[at[ci, h]], v_new[h])
        msq = _mm(jnp.concatenate([x * x for x in o], axis=0), ones_mean)
        for h in hs:
            on = o[h] * lax.rsqrt(msq[h * c:(h + 1) * c] + NORM_EPS) * norm_w
            z = z_ref[ci * c:(ci + 1) * c, h * hd:(h + 1) * hd].astype(F32)
            o_ref[ci * c:(ci + 1) * c, h * hd:(h + 1) * hd] = (on * _silu(z)).astype(o_ref.dtype)

    for part in range(3):
        halo_ref[:, part * w:(part + 1) * w] = refs[part][ts - CONV_HALO:ts, :]


def _gdn(u_main, gb, conv_w, head_params, norm_w, batch, heads, ts):
    t = u_main.shape[0]
    w = heads * GDN_HEAD_DIM
    nt = t // batch // ts

    def col(blk):
        return pl.BlockSpec((ts, w), lambda b, i, blk=blk: (b * nt + i, blk))

    return pl.pallas_call(
        functools.partial(_gdn_kernel, ts=ts, heads=heads),
        grid=(batch, nt),
        in_specs=[
            col(0), col(1), col(2), col(3),
            pl.BlockSpec((ts, LANES), lambda b, i: (b * nt + i, 0)),
            pl.BlockSpec((CONV_K, 3 * w), lambda b, i: (0, 0)),
            pl.BlockSpec((8, LANES), lambda b, i: (0, 0)),
            pl.BlockSpec((1, GDN_HEAD_DIM), lambda b, i: (0, 0)),
        ],
        out_specs=pl.BlockSpec((ts, w), lambda b, i: (b * nt + i, 0)),
        out_shape=jax.ShapeDtypeStruct((t, w), BF16),
        scratch_shapes=[
            pltpu.VMEM((CONV_HALO, 3 * w), BF16),
            pltpu.VMEM((heads, GDN_HEAD_DIM, GDN_HEAD_DIM), F32),
        ],
        compiler_params=_params(("arbitrary", "arbitrary")),
        name="gated_deltanet",
    )(u_main, u_main, u_main, u_main, gb, conv_w, head_params, norm_w)


def _rope_table_kernel(pos_ref, invf_ref, sign_ref, cos_ref, sin_ref):
    ang = pos_ref[...].astype(F32) * invf_ref[...]
    cos_ref[...] = jnp.cos(ang)
    sin_ref[...] = jnp.sin(ang) * sign_ref[...]


def _rope_tables(positions, tr):
    t = positions.size
    half = SWA_HEAD_DIM // 2
    inv_freq = ROPE_THETA ** (-jnp.arange(half, dtype=F32) / half)
    invf = jnp.tile(inv_freq, LANES // half).reshape(1, LANES)
    sign = jnp.tile(jnp.concatenate([-jnp.ones((half,), F32), jnp.ones((half,), F32)]),
                    LANES // SWA_HEAD_DIM).reshape(1, LANES)
    return pl.pallas_call(
        _rope_table_kernel,
        grid=(t // tr,),
        in_specs=[
            pl.BlockSpec((tr, 1), lambda i: (i, 0)),
            pl.BlockSpec((1, LANES), lambda i: (0, 0)),
            pl.BlockSpec((1, LANES), lambda i: (0, 0)),
        ],
        out_specs=[pl.BlockSpec((tr, LANES), lambda i: (i, 0))] * 2,
        out_shape=[jax.ShapeDtypeStruct((t, LANES), F32)] * 2,
        compiler_params=_params(("arbitrary",)),
        name="rope_tables",
    )(positions.reshape(t, 1), invf, sign)


def _swa_kernel(sink_ref, q_ref, z_ref, h_ref, wkv_ref, wgb_ref, cos_ref, sin_ref, o_ref, gb_ref, kprev_ref, vprev_ref,
                *, q_heads, kv_heads):
    blk = pl.program_id(1)
    tq = SWA_BLOCK
    hd = SWA_HEAD_DIM
    group = q_heads // kv_heads
    kvw = kv_heads * hd

    @pl.when(blk == 0)
    def _():
        kprev_ref[...] = jnp.zeros(kprev_ref.shape, kprev_ref.dtype)
        vprev_ref[...] = jnp.zeros(vprev_ref.shape, vprev_ref.dtype)

    cos = cos_ref[...]
    sin = sin_ref[...]
    kv = jnp.dot(h_ref[...], wkv_ref[...], preferred_element_type=F32)
    gb_ref[...] = jnp.dot(h_ref[...], wgb_ref[...], preferred_element_type=F32)
    lane = lax.broadcasted_iota(jnp.int32, (tq, LANES), 1)
    low_half = lane < hd
    first_quarter = (lane & (hd - 1)) < (hd // 2)

    def rope(xs, cos_t, sin_t):
        partner = jnp.where(first_quarter, pltpu.roll(xs, LANES - hd // 2, 1), pltpu.roll(xs, hd // 2, 1))
        return xs * cos_t + partner * sin_t

    q_scale = (hd ** -0.5) * LOG2_E
    cos_q = cos * q_scale
    sin_q = sin * q_scale

    def dup(xs, head_in_slab):
        swapped = pltpu.roll(xs, hd, 1)
        return jnp.where(low_half, xs, swapped) if head_in_slab == 0 else jnp.where(low_half, swapped, xs)

    qi = lax.broadcasted_iota(jnp.int32, (tq, 2 * tq), 0)
    kj = lax.broadcasted_iota(jnp.int32, (tq, 2 * tq), 1)
    lowest = jnp.maximum(qi, jnp.where(blk > 0, -1, tq - 1))
    valid = jnp.logical_and(kj > lowest, kj <= qi + tq)

    kv_per_slab = LANES // hd

    def scores_of_group(g):
        slab = g // kv_per_slab
        k_slab = rope(kv[:, slab * LANES:(slab + 1) * LANES], cos, sin)
        v_slab = kv[:, kvw + slab * LANES:kvw + (slab + 1) * LANES]
        k_cur = dup(k_slab, g % kv_per_slab).astype(BF16)
        v_cur = dup(v_slab, g % kv_per_slab).astype(BF16)
        k_all = jnp.concatenate([kprev_ref[g], k_cur], axis=0)
        v_all = jnp.concatenate([vprev_ref[g], v_cur], axis=0)
        q_parts = []
        for qs in range(g * group // 2, (g + 1) * group // 2):
            q_slab = rope(q_ref[:, qs * LANES:(qs + 1) * LANES].astype(F32), cos_q, sin_q)
            q_parts.append(jnp.where(low_half, q_slab, 0.0).astype(BF16))
            q_parts.append(jnp.where(low_half, 0.0, q_slab).astype(BF16))
        scores = _mm_nt(jnp.concatenate(q_parts, axis=0), k_all)
        return k_cur, v_cur, v_all, scores

    def finish_group(g, k_cur, v_cur, v_all, scores):
        p_parts, inv_denoms = [], []
        for hh in range(group):
            sink = sink_ref[g * group + hh] * LOG2_E
            s = jnp.where(valid, scores[hh * tq:(hh + 1) * tq], MASK_VALUE)
            m = jnp.maximum(jnp.max(s, axis=-1, keepdims=True), sink)
            p = jnp.exp2(s - m)
            denom = jnp.sum(p, axis=-1, keepdims=True) + jnp.exp2(sink - m)
            p_parts.append(p.astype(BF16))
            inv_denoms.append(1.0 / denom)
        pv = jnp.dot(jnp.concatenate(p_parts, axis=0), v_all, preferred_element_type=F32)
        for hh in range(0, group, 2):
            qs = (g * group + hh) // 2
            o_even = pv[hh * tq:(hh + 1) * tq] * inv_denoms[hh]
            o_odd = pv[(hh + 1) * tq:(hh + 2) * tq] * inv_denoms[hh + 1]
            z = z_ref[:, qs * LANES:(qs + 1) * LANES].astype(F32)
            o_ref[:, qs * LANES:(qs + 1) * LANES] = (jnp.where(low_half, o_even, o_odd) * _silu(z)).astype(o_ref.dtype)
        kprev_ref[g] = k_cur
        vprev_ref[g] = v_cur

    pending = scores_of_group(0)
    for g in range(kv_heads):
        current = pending
        if g + 1 < kv_heads:
            pending = scores_of_group(g + 1)
        finish_group(g, *current)


def _swa(u_main, h, w_kv, w_gb, layer, cos_t, sin_t, sinks, batch, q_heads, kv_heads, q_blk, z_blk):
    t, d = h.shape
    w = q_heads * SWA_HEAD_DIM
    nb = t // batch // SWA_BLOCK

    def row(b, i):
        return b * nb + i

    return pl.pallas_call(
        functools.partial(_swa_kernel, q_heads=q_heads, kv_heads=kv_heads),
        grid=(batch, nb),
        in_specs=[
            pl.BlockSpec(memory_space=pltpu.SMEM),
            pl.BlockSpec((SWA_BLOCK, w), lambda b, i: (row(b, i), q_blk)),
            pl.BlockSpec((SWA_BLOCK, w), lambda b, i: (row(b, i), z_blk)),
            pl.BlockSpec((SWA_BLOCK, d), lambda b, i: (row(b, i), 0)),
            pl.BlockSpec((None, d, w_kv.shape[-1]), lambda b, i: (layer, 0, 0)),
            pl.BlockSpec((None, d, LANES), lambda b, i: (layer, 0, 0)),
            pl.BlockSpec((SWA_BLOCK, LANES), lambda b, i: (row(b, i), 0)),
            pl.BlockSpec((SWA_BLOCK, LANES), lambda b, i: (row(b, i), 0)),
        ],
        out_specs=[pl.BlockSpec((SWA_BLOCK, w), lambda b, i: (row(b, i), 0)),
                   pl.BlockSpec((SWA_BLOCK, LANES), lambda b, i: (row(b, i), 0))],
        out_shape=[jax.ShapeDtypeStruct((t, w), BF16), jax.ShapeDtypeStruct((t, LANES), F32)],
        scratch_shapes=[
            pltpu.VMEM((kv_heads, SWA_BLOCK, LANES), BF16),
            pltpu.VMEM((kv_heads, SWA_BLOCK, LANES), BF16),
        ],
        compiler_params=_params(("arbitrary", "arbitrary")),
        name="sliding_window_attention",
    )(sinks, u_main, u_main, h, w_kv, w_gb, cos_t, sin_t)


def _merge_kernel(oa_ref, ob_ref, ga_ref, gb_ref, pa_ref, pb_ref, y_ref):
    ya = jnp.dot(oa_ref[...], pa_ref[...], preferred_element_type=F32)
    yb = jnp.dot(ob_ref[...], pb_ref[...], preferred_element_type=F32)
    y = jax.nn.sigmoid(ga_ref[...].astype(F32)) * ya + jax.nn.sigmoid(gb_ref[...].astype(F32)) * yb
    y_ref[...] = y.astype(y_ref.dtype)


def _merge(o_a, o_b, u_main, proj_a, proj_b, ga_blk, gb_blk, tm):
    t, d = o_a.shape
    resident = dict(pipeline_mode=pl.Buffered(1))
    return pl.pallas_call(
        _merge_kernel,
        grid=(t // tm,),
        in_specs=[
            pl.BlockSpec((tm, d), lambda i: (i, 0)),
            pl.BlockSpec((tm, d), lambda i: (i, 0)),
            pl.BlockSpec((tm, d), lambda i: (i, ga_blk)),
            pl.BlockSpec((tm, d), lambda i: (i, gb_blk)),
            pl.BlockSpec(proj_a.shape, lambda i: (0, 0), **resident),
            pl.BlockSpec(proj_b.shape, lambda i: (0, 0), **resident),
        ],
        out_specs=pl.BlockSpec((tm, d), lambda i: (i, 0)),
        out_shape=jax.ShapeDtypeStruct((t, d), BF16),
        compiler_params=_params(("arbitrary",)),
        name="gated_merge",
    )(o_a, o_b, u_main, u_main, proj_a, proj_b)


def _out_kernel(y_ref, x_ref, w_ref, mod_ref, nmod_ref, nw_ref, *out_refs, final):
    r = jnp.dot(y_ref[...], w_ref[...], preferred_element_type=F32)
    x_new = x_ref[...] + mod_ref[0, 2:3, :] * r
    if final:
        (o_ref,) = out_refs
        o_ref[...] = x_new * lax.rsqrt(jnp.mean(x_new * x_new, axis=-1, keepdims=True) + NORM_EPS) * nw_ref[...]
    else:
        x_out_ref, h_ref = out_refs
        x_out_ref[...] = x_new
        h_ref[...] = _modulated_norm(x_new, nw_ref[...], nmod_ref[0]).astype(h_ref.dtype)


def _out_proj(y, x2, w_out, mod, next_mod, next_nw, batch, tm, final):
    t, d = x2.shape
    nt = t // batch // tm

    def row(b, i):
        return (b * nt + i, 0)

    if final:
        out_specs = pl.BlockSpec((tm, d), row)
        out_shape = jax.ShapeDtypeStruct((t, d), F32)
    else:
        out_specs = [pl.BlockSpec((tm, d), row)] * 2
        out_shape = [jax.ShapeDtypeStruct((t, d), F32), jax.ShapeDtypeStruct((t, d), BF16)]
    return pl.pallas_call(
        functools.partial(_out_kernel, final=final),
        grid=(batch, nt),
        in_specs=[
            pl.BlockSpec((tm, d), row),
            pl.BlockSpec((tm, d), row),
            pl.BlockSpec(w_out.shape, lambda b, i: (0, 0), pipeline_mode=pl.Buffered(1)),
            pl.BlockSpec((1, 3, d), lambda b, i: (b, 0, 0)),
            pl.BlockSpec((1, 3, d), lambda b, i: (b, 0, 0)),
            pl.BlockSpec((1, d), lambda b, i: (0, 0)),
        ],
        out_specs=out_specs,
        out_shape=out_shape,
        compiler_params=_params(("arbitrary", "arbitrary")),
        name="out_proj_final" if final else "out_proj",
    )(y, x2, w_out, mod, next_mod, next_nw)


ROW_ALIGN = 8


def _transpose_kernel(rows_ref, w_ref, o_ref, *, keep):
    del rows_ref
    wt = w_ref[0].T
    if keep < wt.shape[1]:
        col = lax.broadcasted_iota(jnp.int32, wt.shape, 1)
        wt = jnp.where(col < keep, wt, 0.0)
    o_ref[...] = wt.astype(o_ref.dtype)


def _gather_transposed(w_t, src_rows, tc, keep=None):
    depth, _, d = w_t.shape
    n_tiles = len(src_rows)
    grid_spec = pltpu.PrefetchScalarGridSpec(
        num_scalar_prefetch=1,
        grid=(depth, n_tiles),
        in_specs=[pl.BlockSpec((pl.Element(1), pl.Element(tc), pl.Element(d)), lambda l, j, rows: (l, rows[j] * ROW_ALIGN, 0))],
        out_specs=pl.BlockSpec((None, d, tc), lambda l, j, rows: (l, 0, j)),
    )
    return pl.pallas_call(
        functools.partial(_transpose_kernel, keep=tc if keep is None else keep),
        grid_spec=grid_spec,
        out_shape=jax.ShapeDtypeStruct((depth, d, n_tiles * tc), BF16),
        compiler_params=_params(("arbitrary", "arbitrary")),
        name="projection_weight_gather",
    )(jnp.asarray([r // ROW_ALIGN for r in src_rows], jnp.int32), w_t)


def _tile(n, target, align):
    if n <= target:
        return n
    best = align
    for cand in range(align, target + 1, align):
        if n % cand == 0:
            best = cand
    assert n % best == 0
    return best


def kernel(x, c, positions, ada_w, ada_b, norm_w, w_in, conv_w, gdn_a_log, gdn_dt_bias, gdn_norm_w,
           swa_sinks, proj_a, proj_b, w_out, final_norm_w):
    batch, seq, d = x.shape
    depth = ada_w.shape[0]
    t = batch * seq
    gw = conv_w.shape[-1] // 3
    gh = gdn_a_log.shape[-1]
    sw = proj_b.shape[1]
    q_heads = swa_sinks.shape[-1]
    in_cols = w_in.shape[-1]
    kvw = (in_cols - 4 * gw - 2 * gh - 2 * sw - 2 * d) // 2
    kv_heads = kvw // SWA_HEAD_DIM
    assert gw == d and sw == d, "column-block layout assumes both branch widths equal d_model"
    assert gw == gh * GDN_HEAD_DIM and sw == q_heads * SWA_HEAD_DIM
    assert 2 * gh <= LANES and kvw % LANES == 0 and q_heads % (2 * kv_heads) == 0
    assert seq % SWA_BLOCK == 0 and seq % (4 * GDN_CHUNK) == 0 and 2 * GDN_CHUNK == LANES and gh % 2 == 0

    sizes = (3 * gw, gh, gh, gw, sw, kvw, kvw, sw, d, d)
    offs = [0]
    for s in sizes:
        offs.append(offs[-1] + s)
    blk_qb, blk_zb, blk_ga, blk_gb = 4, 5, 6, 7

    src = dict(zip(("qkv_a", "beta_a", "a_a", "z_a", "q_b", "k_b", "v_b", "z_b", "g_a", "g_b"), offs))
    tc = _tile(kvw, 256, LANES)
    assert d % tc == 0 and all(src[name] % ROW_ALIGN == 0 for name in src if name != "a_a")
    main_src = ([src["qkv_a"] + i for i in range(0, 3 * gw, tc)]
                + [src[name] + i for name in ("z_a", "q_b", "z_b", "g_a", "g_b") for i in range(0, d, tc)])
    kv_src = [src["k_b"] + i for i in range(0, kvw, tc)] + [src["v_b"] + i for i in range(0, kvw, tc)]
    w_t = jnp.swapaxes(w_in, 1, 2)
    w_main = _gather_transposed(w_t, main_src, tc)
    w_kv = _gather_transposed(w_t, kv_src, tc)
    w_gb = _gather_transposed(w_t, [src["beta_a"]], LANES, keep=2 * gh)

    head_params = jnp.zeros((depth, 8, LANES), F32)
    head_params = head_params.at[:, 0, gh:2 * gh].set(gdn_a_log.astype(F32))
    head_params = head_params.at[:, 1, gh:2 * gh].set(gdn_dt_bias.astype(F32))

    tm_big = _tile(seq, 1024, 8)
    tm_mid = _tile(seq, 512, 8)
    mod = _modulation(c, ada_w, ada_b, _tile(3 * d, 1024, LANES)).reshape(depth, batch, 3, d)
    cos_t, sin_t = _rope_tables(positions, _tile(t, 2048, 8))

    x2 = x.reshape(t, d)
    h = _prep(x2, mod[0], norm_w[0:1], batch, _tile(seq, 1024, 8))
    out = None
    for l in range(depth):
        u_main = _in_proj_main(h, w_main, l, tm_big, _tile(8 * d, 2048, LANES))
        o_b, gb = _swa(u_main, h, w_kv, w_gb, l, cos_t, sin_t, swa_sinks[l], batch, q_heads, kv_heads, blk_qb, blk_zb)
        o_a = _gdn(u_main, gb, conv_w[l], head_params[l], gdn_norm_w[l:l + 1], batch, gh, 4 * GDN_CHUNK)
        y = _merge(o_a, o_b, u_main, proj_a[l].astype(BF16), proj_b[l].astype(BF16), blk_ga, blk_gb, tm_mid)
        final = l == depth - 1
        if final:
            out = _out_proj(y, x2, w_out[l].astype(BF16), mod[l], mod[l], final_norm_w.reshape(1, d), batch, tm_mid, True)
        else:
            x2, h = _out_proj(y, x2, w_out[l].astype(BF16), mod[l], mod[l + 1], norm_w[l + 1:l + 2], batch, tm_mid, False)
    return out.reshape(batch, seq, d)
```

```python
import functools

import jax
import jax.numpy as jnp
from jax import lax
from jax.experimental import pallas as pl
from jax.experimental.pallas import tpu as pltpu

F32 = jnp.float32
BF16 = jnp.bfloat16

LANES = 128
GDN_HEAD_DIM = 128
GDN_CHUNK = 64
CONV_K = 4
CONV_HALO = 16
SWA_HEAD_DIM = 64
SWA_BLOCK = 128
ROPE_THETA = 10000.0
NORM_EPS = 1e-6
L2_EPS = 1e-6
MASK_VALUE = -1e30
LOG2_E = 1.4426950408889634
VMEM_LIMIT = 56 * 1024 * 1024


def _params(semantics):
    return pltpu.CompilerParams(dimension_semantics=semantics, vmem_limit_bytes=VMEM_LIMIT)


def _mm(a, b):
    return jnp.dot(a.astype(BF16), b.astype(BF16), preferred_element_type=F32)


def _mm_nt(a, b):
    return lax.dot_general(a.astype(BF16), b.astype(BF16), (((1,), (1,)), ((), ())),
                           preferred_element_type=F32)


def _mm_tn(a, b):
    return lax.dot_general(a.astype(BF16), b.astype(BF16), (((0,), (0,)), ((), ())),
                           preferred_element_type=F32)


def _silu(x):
    return x * jax.nn.sigmoid(x)


def _softplus(x):
    return jnp.maximum(x, 0.0) + jnp.log1p(jnp.exp(-jnp.abs(x)))


def _mod_kernel(c_ref, w_ref, b_ref, o_ref):
    c = c_ref[...]
    act = _silu(c)
    hi = act.astype(BF16)
    lo = (act - hi.astype(F32)).astype(BF16)
    w = w_ref[0].astype(BF16)
    acc = jnp.dot(hi, w, preferred_element_type=F32) + jnp.dot(lo, w, preferred_element_type=F32)
    o_ref[0] = acc + b_ref[0]


def _modulation(c, ada_w, ada_b, tn):
    depth, d, n = ada_w.shape
    b = c.shape[0]
    return pl.pallas_call(
        _mod_kernel,
        grid=(depth, n // tn),
        in_specs=[
            pl.BlockSpec((b, d), lambda l, j: (0, 0)),
            pl.BlockSpec((1, d, tn), lambda l, j: (l, 0, j)),
            pl.BlockSpec((1, 1, tn), lambda l, j: (l, 0, j)),
        ],
        out_specs=pl.BlockSpec((1, b, tn), lambda l, j: (l, 0, j)),
        out_shape=jax.ShapeDtypeStruct((depth, b, n), F32),
        compiler_params=_params(("arbitrary", "arbitrary")),
        name="adaln_modulation",
    )(c, ada_w, ada_b.reshape(depth, 1, n))


def _modulated_norm(x, nw, mod):
    y = x * lax.rsqrt(jnp.mean(x * x, axis=-1, keepdims=True) + NORM_EPS) * nw
    return y * (1.0 + mod[1:2, :]) + mod[0:1, :]


def _prep_kernel(x_ref, mod_ref, nw_ref, h_ref):
    h_ref[...] = _modulated_norm(x_ref[...], nw_ref[...], mod_ref[0]).astype(h_ref.dtype)


def _prep(x2, mod, nw, batch, tm):
    t, d = x2.shape
    nt = t // batch // tm
    return pl.pallas_call(
        _prep_kernel,
        grid=(batch, nt),
        in_specs=[
            pl.BlockSpec((tm, d), lambda b, i: (b * nt + i, 0)),
            pl.BlockSpec((1, 3, d), lambda b, i: (b, 0, 0)),
            pl.BlockSpec((1, d), lambda b, i: (0, 0)),
        ],
        out_specs=pl.BlockSpec((tm, d), lambda b, i: (b * nt + i, 0)),
        out_shape=jax.ShapeDtypeStruct((t, d), BF16),
        compiler_params=_params(("arbitrary", "arbitrary")),
        name="modulated_norm",
    )(x2, mod, nw)


def _matmul_kernel(a_ref, w_ref, o_ref):
    o_ref[...] = jnp.dot(a_ref[...], w_ref[...], preferred_element_type=F32).astype(o_ref.dtype)


def _in_proj_main(h, w, layer, tm, tn):
    t, d = h.shape
    n = w.shape[-1]
    return pl.pallas_call(
        _matmul_kernel,
        grid=(n // tn, t // tm),
        in_specs=[
            pl.BlockSpec((tm, d), lambda j, i: (i, 0)),
            pl.BlockSpec((None, d, tn), lambda j, i: (layer, 0, j)),
        ],
        out_specs=pl.BlockSpec((tm, tn), lambda j, i: (i, j)),
        out_shape=jax.ShapeDtypeStruct((t, n), BF16),
        compiler_params=_params(("arbitrary", "arbitrary")),
        name="in_proj_main",
    )(h, w)


INV_BASE = 8


def _pair_iotas(n):
    ii = lax.broadcasted_iota(jnp.int32, (n, 2 * n), 0)
    jj = lax.broadcasted_iota(jnp.int32, (n, 2 * n), 1) & (n - 1)
    return ii, jj


def _inverse_masks(n):
    ii, jj = _pair_iotas(n)
    shift = INV_BASE.bit_length() - 1
    base = (ii >> shift) == (jj >> shift)
    levels = []
    blk = INV_BASE
    while blk < n:
        same_parent = (ii >> (shift + 1)) == (jj >> (shift + 1))
        levels.append(jnp.logical_and(same_parent, (ii >> shift) != (jj >> shift)))
        shift += 1
        blk *= 2
    return base, levels


def _mm_pair(xp, pp, low_half):
    blockdiag = jnp.concatenate([jnp.where(low_half, pp, 0.0), jnp.where(low_half, 0.0, pp)], axis=0)
    return _mm(xp, blockdiag)


def _unit_lower_inverse(lms, eye, masks, low_half):
    base, levels = masks
    ps = [jnp.where(base, lm, 0.0) for lm in lms]
    xs = [eye - p for p in ps]
    step = 2
    while step < INV_BASE:
        ps = [_mm_pair(p, p, low_half) for p in ps]
        xs = [x + _mm_pair(x, p, low_half) for x, p in zip(xs, ps)]
        step *= 2
    for off_mask in levels:
        ts = [_mm_pair(x, jnp.where(off_mask, lm, 0.0), low_half) for x, lm in zip(xs, lms)]
        xs = [x - _mm_pair(t, x, low_half) for x, t in zip(xs, ts)]
    return xs


def _chunk_cumsum(g):
    rows = lax.broadcasted_iota(jnp.int32, g.shape, 0)
    s = 1
    while s < g.shape[0]:
        g = g + jnp.where(rows >= s, pltpu.roll(g, s, 0), 0.0)
        s *= 2
    return g


def _gdn_kernel(q_ref, k_ref, v_ref, z_ref, gb_ref, cw_ref, hp_ref, nw_ref, o_ref,
                halo_ref, state_ref, *, ts, heads):
    c = GDN_CHUNK
    hd = GDN_HEAD_DIM
    w = heads * hd
    t = pl.program_id(1)

    @pl.when(t == 0)
    def _():
        halo_ref[...] = jnp.zeros(halo_ref.shape, halo_ref.dtype)
        state_ref[...] = jnp.zeros(state_ref.shape, F32)

    ii, jj = _pair_iotas(c)
    causal = ii >= jj
    strict = ii > jj
    eye = jnp.where(ii == jj, 1.0, 0.0).astype(F32)
    inv_masks = _inverse_masks(c)
    low_half = lax.broadcasted_iota(jnp.int32, (c, 2 * c), 1) < c
    zeros_hd = jnp.zeros((c, hd), F32)
    norm_w = nw_ref[...]

    n_sh = CONV_K - 1
    rr = lax.broadcasted_iota(jnp.int32, (n_sh * c, CONV_HALO + c), 0)
    cc = lax.broadcasted_iota(jnp.int32, (n_sh * c, CONV_HALO + c), 1)
    log_c = c.bit_length() - 1
    shift_mat = jnp.where(cc == (rr & (c - 1)) + (rr >> log_c) + (CONV_HALO - n_sh), 1.0, 0.0).astype(BF16)
    ones_sum = jnp.ones((hd, hd), BF16)
    ones_mean = jnp.full((hd, hd), 1.0 / hd, BF16)

    refs = (q_ref, k_ref, v_ref)
    n_chunks = ts // c
    items = [(ci, h) for ci in range(n_chunks) for h in range(heads)]
    pair_items = [(ci, h) for ci in range(n_chunks) for h in range(0, heads, 2)]
    at = {it: n for n, it in enumerate(items)}

    beta_all, gc_all, gc_t, egc_all, ekd_all, cdec_all = [], [], [], [], [], []
    for ci in range(n_chunks):
        gb = gb_ref[ci * c:(ci + 1) * c, :]
        beta_all.append(jax.nn.sigmoid(gb))
        gc = _chunk_cumsum(-jnp.exp(hp_ref[0:1, :]) * _softplus(gb + hp_ref[1:2, :]))
        gc_all.append(gc)
        gc_t.append(jnp.concatenate([gc, pltpu.roll(gc, LANES - 1, 1)], axis=0).T)
        glast = gc[c - 1:c, :]
        egc_all.append(jnp.exp(gc))
        ekd_all.append(jnp.exp(glast - gc))
        cdec_all.append(jnp.exp(glast))

    def window(part, ci, h):
        cols = slice(h * hd, (h + 1) * hd)
        if ci == 0:
            prev = halo_ref[:, part * w + h * hd:part * w + (h + 1) * hd]
        else:
            prev = refs[part][ci * c - CONV_HALO:ci * c, cols]
        return jnp.concatenate([prev, refs[part][ci * c:(ci + 1) * c, cols]], axis=0)

    shifted = [jnp.dot(shift_mat, jnp.concatenate([window(p, ci, h) for p in range(3)], axis=1),
                       preferred_element_type=F32) for ci, h in items]

    def conv_silu(part, ci, h):
        col = part * w + h * hd
        acc = refs[part][ci * c:(ci + 1) * c, h * hd:(h + 1) * hd].astype(F32) * cw_ref[n_sh:n_sh + 1, col:col + hd]
        for j in range(n_sh):
            acc = acc + shifted[at[ci, h]][j * c:(j + 1) * c, part * hd:(part + 1) * hd] * cw_ref[j:j + 1, col:col + hd]
        return _silu(acc)

    q = [conv_silu(0, ci, h) for ci, h in items]
    k = [conv_silu(1, ci, h) for ci, h in items]
    v = [conv_silu(2, ci, h) for ci, h in items]
    ssq = _mm(jnp.concatenate([x * x for x in q + k], axis=0), ones_sum)
    n_it = len(items)
    q = [x * lax.rsqrt(ssq[n * c:(n + 1) * c] + L2_EPS) * (hd ** -0.5) for n, x in enumerate(q)]
    k = [x * lax.rsqrt(ssq[(n_it + n) * c:(n_it + n + 1) * c] + L2_EPS) for n, x in enumerate(k)]

    bcol = [beta_all[ci][:, h:h + 1] for ci, h in items]
    gcol = [gc_all[ci][:, heads + h:heads + h + 1] for ci, h in items]
    egcol = [egc_all[ci][:, heads + h:heads + h + 1] for ci, h in items]
    ekdcol = [ekd_all[ci][:, heads + h:heads + h + 1] for ci, h in items]
    cdec = [cdec_all[ci][:, heads + h:heads + h + 1] for ci, h in items]
    kb = [kk * b for kk, b in zip(k, bcol)]
    rhs = [jnp.concatenate([vv * b, kk * e], axis=1) for vv, b, kk, e in zip(v, bcol, kb, egcol)]

    decay = [jnp.where(causal, jnp.exp(jnp.where(causal, jnp.where(low_half, gcol[at[ci, h]], gcol[at[ci, h + 1]])
                                                 - gc_t[ci][heads + h:heads + h + 1, :], 0.0)), 0.0)
             for ci, h in pair_items]
    both = [_mm_nt(jnp.concatenate([jnp.concatenate([kb[at[ci, h]], kb[at[ci, h + 1]]], axis=1),
                                    jnp.concatenate([q[at[ci, h]], q[at[ci, h + 1]]], axis=1)], axis=0),
                   jnp.concatenate([jnp.concatenate([k[at[ci, h]], zeros_hd], axis=1),
                                    jnp.concatenate([zeros_hd, k[at[ci, h + 1]]], axis=1)], axis=0))
            for ci, h in pair_items]
    lm = [jnp.where(strict, b[0:c] * d, 0.0) for b, d in zip(both, decay)]
    qkm_p = [b[c:2 * c] * d for b, d in zip(both, decay)]
    tinv = _unit_lower_inverse(lm, eye, inv_masks, low_half)
    sol_p = [_mm(jnp.concatenate([jnp.where(low_half, tp, 0.0), jnp.where(low_half, 0.0, tp)], axis=0),
                 jnp.concatenate([rhs[at[ci, h]], rhs[at[ci, h + 1]]], axis=0))
             for tp, (ci, h) in zip(tinv, pair_items)]
    q_dec = [qq * e for qq, e in zip(q, egcol)]
    k_dec = [kk * e for kk, e in zip(k, ekdcol)]

    hs = range(heads)
    pairs = range(0, heads, 2)
    for ci in range(n_chunks):
        sol = [sol_p[(ci * heads + h) // 2][(h % 2) * c:(h % 2 + 1) * c] for h in hs]
        s_old = [state_ref[h] for h in hs]
        ws_qs = [_mm(jnp.concatenate([sol[h][:, hd:2 * hd], q_dec[at[ci, h]]], axis=0), s_old[h]) for h in hs]
        v_new = [sol[h][:, 0:hd] - ws_qs[h][0:c] for h in hs]
        o_inter = [ws_qs[h][c:2 * c] for h in hs]
        o_intra_p = [_mm(jnp.concatenate([jnp.where(low_half, qkm_p[(ci * heads + h) // 2], 0.0),
                                          jnp.where(low_half, 0.0, qkm_p[(ci * heads + h) // 2])], axis=0),
                         jnp.concatenate([v_new[h], v_new[h + 1]], axis=0)) for h in pairs]
        o = [o_inter[h] + o_intra_p[h // 2][(h % 2) * c:(h % 2 + 1) * c] for h in hs]
        for h in hs:
            state_ref[h] = s_old[h] * cdec[at[ci, h]] + _mm_tn(k_dec[at[ci, h]], v_new[h])
        msq = _mm(jnp.concatenate([x * x for x in o], axis=0), ones_mean)
        for h in hs:
            on = o[h] * lax.rsqrt(msq[h * c:(h + 1) * c] + NORM_EPS) * norm_w
            z = z_ref[ci * c:(ci + 1) * c, h * hd:(h + 1) * hd].astype(F32)
            o_ref[ci * c:(ci + 1) * c, h * hd:(h + 1) * hd] = (on * _silu(z)).astype(o_ref.dtype)

    for part in range(3):
        halo_ref[:, part * w:(part + 1) * w] = refs[part][ts - CONV_HALO:ts, :]


def _gdn(u_main, gb, conv_w, head_params, norm_w, batch, heads, ts):
    t = u_main.shape[0]
    w = heads * GDN_HEAD_DIM
    nt = t // batch // ts

    def col(blk):
        return pl.BlockSpec((ts, w), lambda b, i, blk=blk: (b * nt + i, blk))

    return pl.pallas_call(
        functools.partial(_gdn_kernel, ts=ts, heads=heads),
        grid=(batch, nt),
        in_specs=[
            col(0), col(1), col(2), col(3),
            pl.BlockSpec((ts, LANES), lambda b, i: (b * nt + i, 0)),
            pl.BlockSpec((CONV_K, 3 * w), lambda b, i: (0, 0)),
            pl.BlockSpec((8, LANES), lambda b, i: (0, 0)),
            pl.BlockSpec((1, GDN_HEAD_DIM), lambda b, i: (0, 0)),
        ],
        out_specs=pl.BlockSpec((ts, w), lambda b, i: (b * nt + i, 0)),
        out_shape=jax.ShapeDtypeStruct((t, w), BF16),
        scratch_shapes=[
            pltpu.VMEM((CONV_HALO, 3 * w), BF16),
            pltpu.VMEM((heads, GDN_HEAD_DIM, GDN_HEAD_DIM), F32),
        ],
        compiler_params=_params(("arbitrary", "arbitrary")),
        name="gated_deltanet",
    )(u_main, u_main, u_main, u_main, gb, conv_w, head_params, norm_w)


def _rope_table_kernel(pos_ref, invf_ref, sign_ref, cos_ref, sin_ref):
    ang = pos_ref[...].astype(F32) * invf_ref[...]
    cos_ref[...] = jnp.cos(ang)
    sin_ref[...] = jnp.sin(ang) * sign_ref[...]


def _rope_tables(positions, tr):
    t = positions.size
    half = SWA_HEAD_DIM // 2
    inv_freq = ROPE_THETA ** (-jnp.arange(half, dtype=F32) / half)
    invf = jnp.tile(inv_freq, LANES // half).reshape(1, LANES)
    sign = jnp.tile(jnp.concatenate([-jnp.ones((half,), F32), jnp.ones((half,), F32)]),
                    LANES // SWA_HEAD_DIM).reshape(1, LANES)
    return pl.pallas_call(
        _rope_table_kernel,
        grid=(t // tr,),
        in_specs=[
            pl.BlockSpec((tr, 1), lambda i: (i, 0)),
            pl.BlockSpec((1, LANES), lambda i: (0, 0)),
            pl.BlockSpec((1, LANES), lambda i: (0, 0)),
        ],
        out_specs=[pl.BlockSpec((tr, LANES), lambda i: (i, 0))] * 2,
        out_shape=[jax.ShapeDtypeStruct((t, LANES), F32)] * 2,
        compiler_params=_params(("arbitrary",)),
        name="rope_tables",
    )(positions.reshape(t, 1), invf, sign)


def _swa_kernel(sink_ref, q_ref, z_ref, h_ref, wkv_ref, wgb_ref, cos_ref, sin_ref, o_ref, gb_ref, kprev_ref, vprev_ref,
                *, q_heads, kv_heads):
    blk = pl.program_id(1)
    tq = SWA_BLOCK
    hd = SWA_HEAD_DIM
    group = q_heads // kv_heads
    kvw = kv_heads * hd

    @pl.when(blk == 0)
    def _():
        kprev_ref[...] = jnp.zeros(kprev_ref.shape, kprev_ref.dtype)
        vprev_ref[...] = jnp.zeros(vprev_ref.shape, vprev_ref.dtype)

    cos = cos_ref[...]
    sin = sin_ref[...]
    kv = jnp.dot(h_ref[...], wkv_ref[...], preferred_element_type=F32)
    gb_ref[...] = jnp.dot(h_ref[...], wgb_ref[...], preferred_element_type=F32)
    lane = lax.broadcasted_iota(jnp.int32, (tq, LANES), 1)
    low_half = lane < hd
    first_quarter = (lane & (hd - 1)) < (hd // 2)

    def rope(xs, cos_t, sin_t):
        partner = jnp.where(first_quarter, pltpu.roll(xs, LANES - hd // 2, 1), pltpu.roll(xs, hd // 2, 1))
        return xs * cos_t + partner * sin_t

    q_scale = (hd ** -0.5) * LOG2_E
    cos_q = cos * q_scale
    sin_q = sin * q_scale

    def dup(xs, head_in_slab):
        swapped = pltpu.roll(xs, hd, 1)
        return jnp.where(low_half, xs, swapped) if head_in_slab == 0 else jnp.where(low_half, swapped, xs)

    qi = lax.broadcasted_iota(jnp.int32, (tq, 2 * tq), 0)
    kj = lax.broadcasted_iota(jnp.int32, (tq, 2 * tq), 1)
    lowest = jnp.maximum(qi, jnp.where(blk > 0, -1, tq - 1))
    valid = jnp.logical_and(kj > lowest, kj <= qi + tq)

    kv_per_slab = LANES // hd

    def scores_of_group(g):
        slab = g // kv_per_slab
        k_slab = rope(kv[:, slab * LANES:(slab + 1) * LANES], cos, sin)
        v_slab = kv[:, kvw + slab * LANES:kvw + (slab + 1) * LANES]
        k_cur = dup(k_slab, g % kv_per_slab).astype(BF16)
        v_cur = dup(v_slab, g % kv_per_slab).astype(BF16)
        k_all = jnp.concatenate([kprev_ref[g], k_cur], axis=0)
        v_all = jnp.concatenate([vprev_ref[g], v_cur], axis=0)
        q_parts = []
        for qs in range(g * group // 2, (g + 1) * group // 2):
            q_slab = rope(q_ref[:, qs * LANES:(qs + 1) * LANES].astype(F32), cos_q, sin_q)
            q_parts.append(jnp.where(low_half, q_slab, 0.0).astype(BF16))
            q_parts.append(jnp.where(low_half, 0.0, q_slab).astype(BF16))
        scores = _mm_nt(jnp.concatenate(q_parts, axis=0), k_all)
        return k_cur, v_cur, v_all, scores

    def finish_group(g, k_cur, v_cur, v_all, scores):
        p_parts, inv_denoms = [], []
        for hh in range(group):
            sink = sink_ref[g * group + hh] * LOG2_E
            s = jnp.where(valid, scores[hh * tq:(hh + 1) * tq], MASK_VALUE)
            m = jnp.maximum(jnp.max(s, axis=-1, keepdims=True), sink)
            p = jnp.exp2(s - m)
            denom = jnp.sum(p, axis=-1, keepdims=True) + jnp.exp2(sink - m)
            p_parts.append(p.astype(BF16))
            inv_denoms.append(1.0 / denom)
        pv = jnp.dot(jnp.concatenate(p_parts, axis=0), v_all, preferred_element_type=F32)
        for hh in range(0, group, 2):
            qs = (g * group + hh) // 2
            o_even = pv[hh * tq:(hh + 1) * tq] * inv_denoms[hh]
            o_odd = pv[(hh + 1) * tq:(hh + 2) * tq] * inv_denoms[hh + 1]
            z = z_ref[:, qs * LANES:(qs + 1) * LANES].astype(F32)
            o_ref[:, qs * LANES:(qs + 1) * LANES] = (jnp.where(low_half, o_even, o_odd) * _silu(z)).astype(o_ref.dtype)
        kprev_ref[g] = k_cur
        vprev_ref[g] = v_cur

    pending = scores_of_group(0)
    for g in range(kv_heads):
        current = pending
        if g + 1 < kv_heads:
            pending = scores_of_group(g + 1)
        finish_group(g, *current)


def _swa(u_main, h, w_kv, w_gb, layer, cos_t, sin_t, sinks, batch, q_heads, kv_heads, q_blk, z_blk):
    t, d = h.shape
    w = q_heads * SWA_HEAD_DIM
    nb = t // batch // SWA_BLOCK

    def row(b, i):
        return b * nb + i

    return pl.pallas_call(
        functools.partial(_swa_kernel, q_heads=q_heads, kv_heads=kv_heads),
        grid=(batch, nb),
        in_specs=[
            pl.BlockSpec(memory_space=pltpu.SMEM),
            pl.BlockSpec((SWA_BLOCK, w), lambda b, i: (row(b, i), q_blk)),
            pl.BlockSpec((SWA_BLOCK, w), lambda b, i: (row(b, i), z_blk)),
            pl.BlockSpec((SWA_BLOCK, d), lambda b, i: (row(b, i), 0)),
            pl.BlockSpec((None, d, w_kv.shape[-1]), lambda b, i: (layer, 0, 0)),
            pl.BlockSpec((None, d, LANES), lambda b, i: (layer, 0, 0)),
            pl.BlockSpec((SWA_BLOCK, LANES), lambda b, i: (row(b, i), 0)),
            pl.BlockSpec((SWA_BLOCK, LANES), lambda b, i: (row(b, i), 0)),
        ],
        out_specs=[pl.BlockSpec((SWA_BLOCK, w), lambda b, i: (row(b, i), 0)),
                   pl.BlockSpec((SWA_BLOCK, LANES), lambda b, i: (row(b, i), 0))],
        out_shape=[jax.ShapeDtypeStruct((t, w), BF16), jax.ShapeDtypeStruct((t, LANES), F32)],
        scratch_shapes=[
            pltpu.VMEM((kv_heads, SWA_BLOCK, LANES), BF16),
            pltpu.VMEM((kv_heads, SWA_BLOCK, LANES), BF16),
        ],
        compiler_params=_params(("arbitrary", "arbitrary")),
        name="sliding_window_attention",
    )(sinks, u_main, u_main, h, w_kv, w_gb, cos_t, sin_t)


def _merge_kernel(oa_ref, ob_ref, ga_ref, gb_ref, pa_ref, pb_ref, y_ref):
    ya = jnp.dot(oa_ref[...], pa_ref[...], preferred_element_type=F32)
    yb = jnp.dot(ob_ref[...], pb_ref[...], preferred_element_type=F32)
    y = jax.nn.sigmoid(ga_ref[...].astype(F32)) * ya + jax.nn.sigmoid(gb_ref[...].astype(F32)) * yb
    y_ref[...] = y.astype(y_ref.dtype)


def _merge(o_a, o_b, u_main, proj_a, proj_b, ga_blk, gb_blk, tm):
    t, d = o_a.shape
    resident = dict(pipeline_mode=pl.Buffered(1))
    return pl.pallas_call(
        _merge_kernel,
        grid=(t // tm,),
        in_specs=[
            pl.BlockSpec((tm, d), lambda i: (i, 0)),
            pl.BlockSpec((tm, d), lambda i: (i, 0)),
            pl.BlockSpec((tm, d), lambda i: (i, ga_blk)),
            pl.BlockSpec((tm, d), lambda i: (i, gb_blk)),
            pl.BlockSpec(proj_a.shape, lambda i: (0, 0), **resident),
            pl.BlockSpec(proj_b.shape, lambda i: (0, 0), **resident),
        ],
        out_specs=pl.BlockSpec((tm, d), lambda i: (i, 0)),
        out_shape=jax.ShapeDtypeStruct((t, d), BF16),
        compiler_params=_params(("arbitrary",)),
        name="gated_merge",
    )(o_a, o_b, u_main, u_main, proj_a, proj_b)


def _out_kernel(y_ref, x_ref, w_ref, mod_ref, nmod_ref, nw_ref, *out_refs, final):
    r = jnp.dot(y_ref[...], w_ref[...], preferred_element_type=F32)
    x_new = x_ref[...] + mod_ref[0, 2:3, :] * r
    if final:
        (o_ref,) = out_refs
        o_ref[...] = x_new * lax.rsqrt(jnp.mean(x_new * x_new, axis=-1, keepdims=True) + NORM_EPS) * nw_ref[...]
    else:
        x_out_ref, h_ref = out_refs
        x_out_ref[...] = x_new
        h_ref[...] = _modulated_norm(x_new, nw_ref[...], nmod_ref[0]).astype(h_ref.dtype)


def _out_proj(y, x2, w_out, mod, next_mod, next_nw, batch, tm, final):
    t, d = x2.shape
    nt = t // batch // tm

    def row(b, i):
        return (b * nt + i, 0)

    if final:
        out_specs = pl.BlockSpec((tm, d), row)
        out_shape = jax.ShapeDtypeStruct((t, d), F32)
    else:
        out_specs = [pl.BlockSpec((tm, d), row)] * 2
        out_shape = [jax.ShapeDtypeStruct((t, d), F32), jax.ShapeDtypeStruct((t, d), BF16)]
    return pl.pallas_call(
        functools.partial(_out_kernel, final=final),
        grid=(batch, nt),
        in_specs=[
            pl.BlockSpec((tm, d), row),
            pl.BlockSpec((tm, d), row),
            pl.BlockSpec(w_out.shape, lambda b, i: (0, 0), pipeline_mode=pl.Buffered(1)),
            pl.BlockSpec((1, 3, d), lambda b, i: (b, 0, 0)),
            pl.BlockSpec((1, 3, d), lambda b, i: (b, 0, 0)),
            pl.BlockSpec((1, d), lambda b, i: (0, 0)),
        ],
        out_specs=out_specs,
        out_shape=out_shape,
        compiler_params=_params(("arbitrary", "arbitrary")),
        name="out_proj_final" if final else "out_proj",
    )(y, x2, w_out, mod, next_mod, next_nw)


def _merge_out_kernel(oa_ref, ob_ref, ga_ref, gb_ref, x_ref, pa_ref, pb_ref, wo_ref, mod_ref, nmod_ref, nw_ref,
                      *out_refs, final):
    ya = jnp.dot(oa_ref[...], pa_ref[...], preferred_element_type=F32)
    yb = jnp.dot(ob_ref[...], pb_ref[...], preferred_element_type=F32)
    y = jax.nn.sigmoid(ga_ref[...].astype(F32)) * ya + jax.nn.sigmoid(gb_ref[...].astype(F32)) * yb
    r = jnp.dot(y.astype(BF16), wo_ref[...], preferred_element_type=F32)
    x_new = x_ref[...] + mod_ref[0, 2:3, :] * r
    if final:
        (o_ref,) = out_refs
        o_ref[...] = x_new * lax.rsqrt(jnp.mean(x_new * x_new, axis=-1, keepdims=True) + NORM_EPS) * nw_ref[...]
    else:
        x_out_ref, h_ref = out_refs
        x_out_ref[...] = x_new
        h_ref[...] = _modulated_norm(x_new, nw_ref[...], nmod_ref[0]).astype(h_ref.dtype)


def _merge_out(o_a, o_b, u_main, x2, proj_a, proj_b, w_out, mod, next_mod, next_nw, ga_blk, gb_blk, batch, tm, final):
    t, d = x2.shape
    nt = t // batch // tm

    def row(b, i):
        return (b * nt + i, 0)

    def resident(shape):
        return pl.BlockSpec(shape, lambda b, i: (0, 0), pipeline_mode=pl.Buffered(1))

    if final:
        out_specs = pl.BlockSpec((tm, d), row)
        out_shape = jax.ShapeDtypeStruct((t, d), F32)
    else:
        out_specs = [pl.BlockSpec((tm, d), row)] * 2
        out_shape = [jax.ShapeDtypeStruct((t, d), F32), jax.ShapeDtypeStruct((t, d), BF16)]
    return pl.pallas_call(
        functools.partial(_merge_out_kernel, final=final),
        grid=(batch, nt),
        in_specs=[
            pl.BlockSpec((tm, d), row),
            pl.BlockSpec((tm, d), row),
            pl.BlockSpec((tm, d), lambda b, i: (b * nt + i, ga_blk)),
            pl.BlockSpec((tm, d), lambda b, i: (b * nt + i, gb_blk)),
            pl.BlockSpec((tm, d), row),
            resident(proj_a.shape), resident(proj_b.shape), resident(w_out.shape),
            pl.BlockSpec((1, 3, d), lambda b, i: (b, 0, 0)),
            pl.BlockSpec((1, 3, d), lambda b, i: (b, 0, 0)),
            pl.BlockSpec((1, d), lambda b, i: (0, 0)),
        ],
        out_specs=out_specs,
        out_shape=out_shape,
        compiler_params=_params(("arbitrary", "arbitrary")),
        name="merge_out_final" if final else "merge_out",
    )(o_a, o_b, u_main, u_main, x2, proj_a, proj_b, w_out, mod, next_mod, next_nw)


ROW_ALIGN = 8


def _transpose_kernel(rows_ref, w_ref, o_ref, *, keep):
    del rows_ref
    wt = w_ref[0].T
    if keep < wt.shape[1]:
        col = lax.broadcasted_iota(jnp.int32, wt.shape, 1)
        wt = jnp.where(col < keep, wt, 0.0)
    o_ref[...] = wt.astype(o_ref.dtype)


def _gather_transposed(w_t, src_rows, tc, keep=None):
    depth, _, d = w_t.shape
    n_tiles = len(src_rows)
    grid_spec = pltpu.PrefetchScalarGridSpec(
        num_scalar_prefetch=1,
        grid=(depth, n_tiles),
        in_specs=[pl.BlockSpec((pl.Element(1), pl.Element(tc), pl.Element(d)), lambda l, j, rows: (l, rows[j] * ROW_ALIGN, 0))],
        out_specs=pl.BlockSpec((None, d, tc), lambda l, j, rows: (l, 0, j)),
    )
    return pl.pallas_call(
        functools.partial(_transpose_kernel, keep=tc if keep is None else keep),
        grid_spec=grid_spec,
        out_shape=jax.ShapeDtypeStruct((depth, d, n_tiles * tc), BF16),
        compiler_params=_params(("arbitrary", "arbitrary")),
        name="projection_weight_gather",
    )(jnp.asarray([r // ROW_ALIGN for r in src_rows], jnp.int32), w_t)


def _tile(n, target, align):
    if n <= target:
        return n
    best = align
    for cand in range(align, target + 1, align):
        if n % cand == 0:
            best = cand
    assert n % best == 0
    return best


def kernel(x, c, positions, ada_w, ada_b, norm_w, w_in, conv_w, gdn_a_log, gdn_dt_bias, gdn_norm_w,
           swa_sinks, proj_a, proj_b, w_out, final_norm_w):
    batch, seq, d = x.shape
    depth = ada_w.shape[0]
    t = batch * seq
    gw = conv_w.shape[-1] // 3
    gh = gdn_a_log.shape[-1]
    sw = proj_b.shape[1]
    q_heads = swa_sinks.shape[-1]
    in_cols = w_in.shape[-1]
    kvw = (in_cols - 4 * gw - 2 * gh - 2 * sw - 2 * d) // 2
    kv_heads = kvw // SWA_HEAD_DIM
    assert gw == d and sw == d, "column-block layout assumes both branch widths equal d_model"
    assert gw == gh * GDN_HEAD_DIM and sw == q_heads * SWA_HEAD_DIM
    assert 2 * gh <= LANES and kvw % LANES == 0 and q_heads % (2 * kv_heads) == 0
    assert seq % SWA_BLOCK == 0 and seq % (4 * GDN_CHUNK) == 0 and 2 * GDN_CHUNK == LANES and gh % 2 == 0

    sizes = (3 * gw, gh, gh, gw, sw, kvw, kvw, sw, d, d)
    offs = [0]
    for s in sizes:
        offs.append(offs[-1] + s)
    blk_qb, blk_zb, blk_ga, blk_gb = 4, 5, 6, 7

    src = dict(zip(("qkv_a", "beta_a", "a_a", "z_a", "q_b", "k_b", "v_b", "z_b", "g_a", "g_b"), offs))
    tc = _tile(kvw, 256, LANES)
    assert d % tc == 0 and all(src[name] % ROW_ALIGN == 0 for name in src if name != "a_a")
    main_src = ([src["qkv_a"] + i for i in range(0, 3 * gw, tc)]
                + [src[name] + i for name in ("z_a", "q_b", "z_b", "g_a", "g_b") for i in range(0, d, tc)])
    kv_src = [src["k_b"] + i for i in range(0, kvw, tc)] + [src["v_b"] + i for i in range(0, kvw, tc)]
    w_t = jnp.swapaxes(w_in, 1, 2)
    w_main = _gather_transposed(w_t, main_src, tc)
    w_kv = _gather_transposed(w_t, kv_src, tc)
    w_gb = _gather_transposed(w_t, [src["beta_a"]], LANES, keep=2 * gh)

    head_params = jnp.zeros((depth, 8, LANES), F32)
    head_params = head_params.at[:, 0, gh:2 * gh].set(gdn_a_log.astype(F32))
    head_params = head_params.at[:, 1, gh:2 * gh].set(gdn_dt_bias.astype(F32))

    tm_big = _tile(seq, 1024, 8)
    tm_mid = _tile(seq, 256, 8)
    mod = _modulation(c, ada_w, ada_b, _tile(3 * d, 1024, LANES)).reshape(depth, batch, 3, d)
    cos_t, sin_t = _rope_tables(positions, _tile(t, 2048, 8))

    x2 = x.reshape(t, d)
    h = _prep(x2, mod[0], norm_w[0:1], batch, _tile(seq, 1024, 8))
    out = None
    for l in range(depth):
        u_main = _in_proj_main(h, w_main, l, tm_big, _tile(8 * d, 2048, LANES))
        o_b, gb = _swa(u_main, h, w_kv, w_gb, l, cos_t, sin_t, swa_sinks[l], batch, q_heads, kv_heads, blk_qb, blk_zb)
        o_a = _gdn(u_main, gb, conv_w[l], head_params[l], gdn_norm_w[l:l + 1], batch, gh, 4 * GDN_CHUNK)
        weights = (proj_a[l].astype(BF16), proj_b[l].astype(BF16), w_out[l].astype(BF16))
        if l == depth - 1:
            out = _merge_out(o_a, o_b, u_main, x2, *weights, mod[l], mod[l], final_norm_w.reshape(1, d),
                             blk_ga, blk_gb, batch, tm_mid, True)
        else:
            x2, h = _merge_out(o_a, o_b, u_main, x2, *weights, mod[l], mod[l + 1], norm_w[l + 1:l + 2],
                               blk_ga, blk_gb, batch, tm_mid, False)
    return out.reshape(batch, seq, d)
```
